```python
import math
import jax, jax.numpy as jnp
from jax import lax
import numpy as np

D_MODEL = 1024
BATCH = 8
SEQ = 4096
DEPTH = 2

N_HEADS = 8
HEAD_DIM = 64
V_DIM = 2 * HEAD_DIM
ATTN_WIDTH = N_HEADS * 2 * HEAD_DIM
C_CONV = D_MODEL
CONV_K = 31
D_FF = 256 * ((8 * D_MODEL // 3 + 255) // 256)
Q_BLOCK = 128
EPS = 1e-6
IN_COLS = 2 * C_CONV + 3 * ATTN_WIDTH + 2 * D_MODEL
SPLIT_POINTS = [2 * C_CONV,
                2 * C_CONV + ATTN_WIDTH,
                2 * C_CONV + 2 * ATTN_WIDTH,
                2 * C_CONV + 3 * ATTN_WIDTH,
                2 * C_CONV + 3 * ATTN_WIDTH + D_MODEL]

kernel_name = "hybrid_conformer_conv_diff_attn_macaron"


def rms_norm(x, g):
    xf = x.astype(jnp.float32)
    y = xf * lax.rsqrt(jnp.mean(xf * xf, axis=-1, keepdims=True) + EPS)
    return (y * g.astype(jnp.float32)).astype(x.dtype)


def layer_norm(x, g, b):
    xf = x.astype(jnp.float32)
    mu = jnp.mean(xf, axis=-1, keepdims=True)
    xc = xf - mu
    var = jnp.mean(xc * xc, axis=-1, keepdims=True)
    y = xc * lax.rsqrt(var + EPS) * g.astype(jnp.float32) + b.astype(jnp.float32)
    return y.astype(x.dtype)


def swiglu_ffn(h, w_in, w_out):
    a, b = jnp.split(h @ w_in, 2, axis=-1)
    return (jax.nn.silu(a) * b) @ w_out


def conv_module(u_pre, dw, dw_b, ln_g, ln_b, w_out):
    a, gte = jnp.split(u_pre, 2, axis=-1)
    u = a * jax.nn.sigmoid(gte)
    y = lax.conv_general_dilated(
        u, dw[:, None, :].astype(u.dtype), window_strides=(1,),
        padding=[(CONV_K - 1, 0)], dimension_numbers=("NWC", "WIO", "NWC"),
        feature_group_count=u.shape[-1]) + dw_b
    y = jax.nn.silu(layer_norm(y, ln_g, ln_b))
    return y @ w_out


def diff_attention(q, k, v, q_norm, k_norm, lam_q, lam_k, subln, w_o, lambda_init):
    B, T, _ = q.shape
    q = q.reshape(B, T, N_HEADS, 2, HEAD_DIM)
    k = k.reshape(B, T, N_HEADS, 2, HEAD_DIM)
    v = v.reshape(B, T, N_HEADS, V_DIM)
    q = rms_norm(q, q_norm) * (HEAD_DIM ** -0.5)
    k = rms_norm(k, k_norm)
    lq = lam_q.astype(jnp.float32)
    lk = lam_k.astype(jnp.float32)
    lam = (jnp.exp(jnp.sum(lq[0] * lk[0])) - jnp.exp(jnp.sum(lq[1] * lk[1]))
           + lambda_init)
    outs = []
    for i in range(T // Q_BLOCK):
        start = i * Q_BLOCK
        L = start + Q_BLOCK
        qb = q[:, start:L]
        kb = k[:, :L]
        vb = v[:, :L]
        s = jnp.einsum("bqhcd,bkhcd->bhcqk", qb, kb).astype(jnp.float32)
        q_pos = start + jnp.arange(Q_BLOCK)[:, None]
        k_pos = jnp.arange(L)[None, :]
        s = jnp.where(k_pos <= q_pos, s, -jnp.inf)
        p = jax.nn.softmax(s, axis=-1)
        w = p[:, :, 0] - lam * p[:, :, 1]
        outs.append(jnp.einsum("bhqk,bkhe->bqhe", w.astype(vb.dtype), vb))
    o = jnp.concatenate(outs, axis=1)
    o = rms_norm(o, subln) * (1.0 - lambda_init)
    return o.reshape(B, T, ATTN_WIDTH) @ w_o


def setup_inputs(seed: int = 0) -> dict:
    key = jax.random.key(seed)
    ks = jax.random.split(key, 24)
    f32 = jnp.float32

    def w(k, shape, fan_in):
        return jax.random.normal(k, shape, f32) * (fan_in ** -0.5)

    def gain(k, shape):
        return 1.0 + 0.02 * jax.random.normal(k, shape, f32)

    def bias(k, shape):
        return 0.02 * jax.random.normal(k, shape, f32)

    return {
        "x": jax.random.normal(ks[0], (BATCH, SEQ, D_MODEL), f32),
        "ffn1_norm": gain(ks[1], (DEPTH, D_MODEL)),
        "ffn1_w_in": w(ks[2], (DEPTH, D_MODEL, 2 * D_FF), D_MODEL),
        "ffn1_w_out": w(ks[3], (DEPTH, D_FF, D_MODEL), D_FF),
        "mix_norm": gain(ks[4], (DEPTH, D_MODEL)),
        "w_in": w(ks[5], (DEPTH, D_MODEL, IN_COLS), D_MODEL),
        "conv_dw": w(ks[6], (DEPTH, CONV_K, C_CONV), CONV_K),
        "conv_dw_b": bias(ks[7], (DEPTH, C_CONV)),
        "conv_ln_g": gain(ks[8], (DEPTH, C_CONV)),
        "conv_ln_b": bias(ks[9], (DEPTH, C_CONV)),
        "conv_w_out": w(ks[10], (DEPTH, C_CONV, D_MODEL), C_CONV),
        "q_norm": gain(ks[11], (DEPTH, HEAD_DIM)),
        "k_norm": gain(ks[12], (DEPTH, HEAD_DIM)),
        "lam_q": 0.1 * jax.random.normal(ks[13], (DEPTH, 2, HEAD_DIM), f32),
        "lam_k": 0.1 * jax.random.normal(ks[14], (DEPTH, 2, HEAD_DIM), f32),
        "attn_subln": gain(ks[15], (DEPTH, V_DIM)),
        "attn_w_out": w(ks[16], (DEPTH, ATTN_WIDTH, D_MODEL), ATTN_WIDTH),
        "w_out": w(ks[17], (DEPTH, D_MODEL, D_MODEL), D_MODEL),
        "ffn2_norm": gain(ks[18], (DEPTH, D_MODEL)),
        "ffn2_w_in": w(ks[19], (DEPTH, D_MODEL, 2 * D_FF), D_MODEL),
        "ffn2_w_out": w(ks[20], (DEPTH, D_FF, D_MODEL), D_FF),
    }


def reference(x, ffn1_norm, ffn1_w_in, ffn1_w_out, mix_norm, w_in, conv_dw, conv_dw_b,
              conv_ln_g, conv_ln_b, conv_w_out, q_norm, k_norm, lam_q, lam_k,
              attn_subln, attn_w_out, w_out, ffn2_norm, ffn2_w_in, ffn2_w_out):
    for l in range(DEPTH):
        lambda_init = 0.8 - 0.6 * math.exp(-0.3 * l)
        x = x + 0.5 * swiglu_ffn(rms_norm(x, ffn1_norm[l]), ffn1_w_in[l], ffn1_w_out[l])
        h = rms_norm(x, mix_norm[l])
        proj = h @ w_in[l]
        u_pre, q, k, v, g_conv, g_attn = jnp.split(proj, SPLIT_POINTS, axis=-1)
        y_conv = conv_module(u_pre, conv_dw[l], conv_dw_b[l], conv_ln_g[l], conv_ln_b[l],
                             conv_w_out[l])
        y_attn = diff_attention(q, k, v, q_norm[l], k_norm[l], lam_q[l], lam_k[l],
                                attn_subln[l], attn_w_out[l], lambda_init)
        merged = jax.nn.sigmoid(g_conv) * y_conv + jax.nn.sigmoid(g_attn) * y_attn
        x = x + merged @ w_out[l]
        x = x + 0.5 * swiglu_ffn(rms_norm(x, ffn2_norm[l]), ffn2_w_in[l], ffn2_w_out[l])
    return x
```

```python
import functools
import math

import jax
import jax.numpy as jnp
from jax import lax
from jax.experimental import pallas as pl
from jax.experimental.pallas import tpu as pltpu

D_MODEL = 1024
N_HEADS = 8
HEAD_DIM = 64
V_DIM = 2 * HEAD_DIM
CONV_K = 31
EPS = 1e-6

VMEM_LIMIT_BYTES = 56 * 1024 * 1024
FFN_CHUNK = 256
HALO = 32

F32 = jnp.float32
BF16 = jnp.bfloat16


def _params(*sem):
    return pltpu.CompilerParams(dimension_semantics=sem, vmem_limit_bytes=VMEM_LIMIT_BYTES)


def _const_spec(shape):
    nd = len(shape)
    return pl.BlockSpec(shape, lambda *_: (0,) * nd, pipeline_mode=pl.Buffered(1))


def _rms_rows(x, g):
    ms = jnp.mean(x * x, axis=-1, keepdims=True)
    return x * lax.rsqrt(ms + EPS) * g


def _dot(a, b):
    return jnp.dot(a, b, preferred_element_type=F32)


def _ffn_kernel(x_ref, g_ref, wa_ref, wb_ref, wo_ref, o_ref, acc_ref, *, d_ff):
    x = x_ref[...]
    h = _rms_rows(x, g_ref[...]).astype(BF16)
    for c in range(d_ff // FFN_CHUNK):
        sl = slice(c * FFN_CHUNK, (c + 1) * FFN_CHUNK)
        a = _dot(h, wa_ref[:, sl])
        b = _dot(h, wb_ref[:, sl])
        gated = (a * jax.nn.sigmoid(a) * b).astype(BF16)
        contrib = _dot(gated, wo_ref[sl, :])
        if c == 0:
            acc_ref[...] = contrib
        else:
            acc_ref[...] += contrib
    o_ref[...] = x + 0.5 * acc_ref[...]


def _ffn(x, g, w_in, w_out, *, tm):
    n, d = x.shape
    d_ff = w_out.shape[0]
    row = pl.BlockSpec((tm, d), lambda i: (i, 0))
    return pl.pallas_call(
        functools.partial(_ffn_kernel, d_ff=d_ff),
        grid=(n // tm,),
        in_specs=[
            row,
            _const_spec((1, d)),
            pl.BlockSpec((d, d_ff), lambda i: (0, 0), pipeline_mode=pl.Buffered(1)),
            pl.BlockSpec((d, d_ff), lambda i: (0, 1), pipeline_mode=pl.Buffered(1)),
            _const_spec((d_ff, d)),
        ],
        out_specs=row,
        out_shape=jax.ShapeDtypeStruct((n, d), F32),
        scratch_shapes=[pltpu.VMEM((tm, d), F32)],
        compiler_params=_params("parallel"),
        name="ffn",
    )(x, g, w_in, w_in, w_out)


def _qk_norm(p, gain):
    outs = []
    for hh in range(N_HEADS):
        slab = p[:, hh * V_DIM:(hh + 1) * V_DIM]
        sq = slab * slab
        lane = lax.broadcasted_iota(jnp.int32, slab.shape, 1)
        first = lane < HEAD_DIM
        s0 = jnp.sum(jnp.where(first, sq, 0.0), axis=-1, keepdims=True)
        s1 = jnp.sum(jnp.where(first, 0.0, sq), axis=-1, keepdims=True)
        inv = jnp.where(first, lax.rsqrt(s0 / HEAD_DIM + EPS), lax.rsqrt(s1 / HEAD_DIM + EPS))
        outs.append(slab * inv * gain[:, hh * V_DIM:(hh + 1) * V_DIM])
    return outs


def _proj_kernel(x_ref, g_ref, w_ref, qg_ref, kg_ref,
                 u_ref, q_ref, k_ref, v_ref, gc_ref, ga_ref):
    d = D_MODEL
    h = _rms_rows(x_ref[...], g_ref[...]).astype(BF16)
    a = _dot(h, w_ref[:, 0:d])
    gte = _dot(h, w_ref[:, d:2 * d])
    u_ref[...] = a * jax.nn.sigmoid(gte)
    q = _dot(h, w_ref[:, 2 * d:3 * d])
    for hh, qh in enumerate(_qk_norm(q, qg_ref[...])):
        q_ref[:, hh * V_DIM:(hh + 1) * V_DIM] = (qh * (HEAD_DIM ** -0.5)).astype(BF16)
    k = _dot(h, w_ref[:, 3 * d:4 * d])
    for hh, kh in enumerate(_qk_norm(k, kg_ref[...])):
        k_ref[:, hh * V_DIM:(hh + 1) * V_DIM] = kh.astype(BF16)
    v_ref[...] = _dot(h, w_ref[:, 4 * d:5 * d]).astype(BF16)
    gc_ref[...] = jax.nn.sigmoid(_dot(h, w_ref[:, 5 * d:6 * d]))
    ga_ref[...] = jax.nn.sigmoid(_dot(h, w_ref[:, 6 * d:7 * d]))


def _proj(x, g, w, qg, kg, *, tm):
    n, d = x.shape
    row = pl.BlockSpec((tm, d), lambda i: (i, 0))
    f32o = jax.ShapeDtypeStruct((n, d), F32)
    b16o = jax.ShapeDtypeStruct((n, d), BF16)
    return pl.pallas_call(
        _proj_kernel,
        grid=(n // tm,),
        in_specs=[row, _const_spec((1, d)), _const_spec(w.shape),
                  _const_spec((1, d)), _const_spec((1, d))],
        out_specs=[row] * 6,
        out_shape=[f32o, b16o, b16o, b16o, f32o, f32o],
        compiler_params=_params("parallel"),
        name="proj",
    )(x, g, w, qg, kg)


def _conv_kernel(u_ref, halo_ref, dw_ref, b_ref, lg_ref, lb_ref, o_ref, win_ref, *, tt, rc):
    j = pl.program_id(1)
    win_ref[0:HALO, :] = jnp.where(j > 0, halo_ref[...], 0.0)
    win_ref[HALO:HALO + tt, :] = u_ref[...]
    first = HALO - (CONV_K - 1)
    for c in range(tt // rc):
        r0 = c * rc
        acc = jnp.broadcast_to(b_ref[...], (rc, D_MODEL))
        for tap in range(CONV_K):
            lo = r0 + first + tap
            acc = acc + dw_ref[tap:tap + 1, :] * win_ref[lo:lo + rc, :]
        mu = jnp.mean(acc, axis=-1, keepdims=True)
        xc = acc - mu
        var = jnp.mean(xc * xc, axis=-1, keepdims=True)
        y = xc * lax.rsqrt(var + EPS) * lg_ref[...] + lb_ref[...]
        o_ref[r0:r0 + rc, :] = (y * jax.nn.sigmoid(y)).astype(BF16)


def _conv(u, dw, dw_b, ln_g, ln_b, *, batch, seq, tt, rc):
    d = u.shape[-1]
    u3 = u.reshape(batch, seq, d)
    per = tt // HALO
    cur = pl.BlockSpec((None, tt, d), lambda b, j: (b, j, 0))
    halo = pl.BlockSpec((None, HALO, d), lambda b, j: (b, jnp.maximum(j * per - 1, 0), 0))
    vec = pl.BlockSpec((1, d), lambda b, j: (0, 0))
    out = pl.pallas_call(
        functools.partial(_conv_kernel, tt=tt, rc=rc),
        grid=(batch, seq // tt),
        in_specs=[cur, halo, pl.BlockSpec((CONV_K, d), lambda b, j: (0, 0)), vec, vec, vec],
        out_specs=cur,
        out_shape=jax.ShapeDtypeStruct((batch, seq, d), BF16),
        scratch_shapes=[pltpu.VMEM((HALO + tt, d), F32)],
        compiler_params=_params("parallel", "parallel"),
        name="conv",
    )(u3, u3, dw, dw_b, ln_g, ln_b)
    return out.reshape(batch * seq, d)


def _attn_kernel(q_ref, k_ref, v_ref, lq_ref, lk_ref, sub_ref, o_ref, m_ref, l_ref, acc_ref,
                 *, tq, lambda_init):
    i = pl.program_id(2)
    q = q_ref[...]
    lane = lax.broadcasted_iota(jnp.int32, q.shape, 1)
    zero = jnp.zeros_like(q)
    qs = jnp.concatenate([jnp.where(lane < HEAD_DIM, q, zero),
                          jnp.where(lane < HEAD_DIM, zero, q)], axis=0)
    m_ref[...] = jnp.full(m_ref.shape, -jnp.inf, F32)
    l_ref[...] = jnp.zeros(l_ref.shape, F32)
    acc_ref[...] = jnp.zeros(acc_ref.shape, F32)

    def step(j, masked):
        start = pl.multiple_of(j * tq, tq)
        kt = k_ref[pl.ds(start, tq), :]
        vt = v_ref[pl.ds(start, tq), :]
        s = lax.dot_general(qs, kt, (((1,), (1,)), ((), ())), preferred_element_type=F32)
        if masked:
            row = lax.broadcasted_iota(jnp.int32, s.shape, 0)
            row = jnp.where(row >= tq, row - tq, row)
            col = lax.broadcasted_iota(jnp.int32, s.shape, 1)
            s = jnp.where(col <= row, s, -jnp.inf)
        m_old = m_ref[...]
        m_new = jnp.maximum(m_old, jnp.max(s, axis=-1, keepdims=True))
        alpha = jnp.exp(m_old - m_new)
        p = jnp.exp(s - m_new)
        l_ref[...] = alpha * l_ref[...] + jnp.sum(p, axis=-1, keepdims=True)
        acc_ref[...] = alpha * acc_ref[...] + _dot(p.astype(BF16), vt)
        m_ref[...] = m_new

    def body(j, carry):
        step(j, False)
        return carry

    lax.fori_loop(0, i, body, 0)
    step(i, True)

    lq = lq_ref[...]
    lk = lk_ref[...]
    lam = (jnp.exp(jnp.sum(lq[0:1] * lk[0:1], axis=-1, keepdims=True))
           - jnp.exp(jnp.sum(lq[1:2] * lk[1:2], axis=-1, keepdims=True)) + lambda_init)
    o = acc_ref[...] / l_ref[...]
    o = o[0:tq] - lam * o[tq:2 * tq]
    o_ref[...] = (_rms_rows(o, sub_ref[...]) * (1.0 - lambda_init)).astype(BF16)


def _attn(q, k, v, lam_q, lam_k, subln, *, batch, seq, tq, lambda_init):
    d = q.shape[-1]
    q3, k3, v3 = (t.reshape(batch, seq, d) for t in (q, k, v))
    qspec = pl.BlockSpec((None, tq, V_DIM), lambda b, h, i: (b, i, h))
    kvspec = pl.BlockSpec((None, seq, V_DIM), lambda b, h, i: (b, 0, h))
    small = lambda shape: pl.BlockSpec(shape, lambda b, h, i: (0, 0))
    out = pl.pallas_call(
        functools.partial(_attn_kernel, tq=tq, lambda_init=lambda_init),
        grid=(batch, N_HEADS, seq // tq),
        in_specs=[qspec, kvspec, kvspec, small((2, HEAD_DIM)), small((2, HEAD_DIM)),
                  small((1, V_DIM))],
        out_specs=qspec,
        out_shape=jax.ShapeDtypeStruct((batch, seq, d), BF16),
        scratch_shapes=[pltpu.VMEM((2 * tq, 1), F32), pltpu.VMEM((2 * tq, 1), F32),
                        pltpu.VMEM((2 * tq, V_DIM), F32)],
        compiler_params=_params("parallel", "parallel", "arbitrary"),
        name="attn",
    )(q3, k3, v3, lam_q, lam_k, subln)
    return out.reshape(batch * seq, d)


def _merge_kernel(x_ref, c_ref, a_ref, gc_ref, ga_ref, wc_ref, wa_ref, wo_ref, o_ref):
    yc = _dot(c_ref[...], wc_ref[...])
    ya = _dot(a_ref[...], wa_ref[...])
    merged = gc_ref[...] * yc + ga_ref[...] * ya
    o_ref[...] = x_ref[...] + _dot(merged.astype(BF16), wo_ref[...])


def _merge(x, conv_act, attn_o, gc, ga, wc, wa, wo, *, tm):
    n, d = x.shape
    row = pl.BlockSpec((tm, d), lambda i: (i, 0))
    w = _const_spec((d, d))
    return pl.pallas_call(
        _merge_kernel,
        grid=(n // tm,),
        in_specs=[row] * 5 + [w, w, w],
        out_specs=row,
        out_shape=jax.ShapeDtypeStruct((n, d), F32),
        compiler_params=_params("parallel"),
        name="merge",
    )(x, conv_act, attn_o, gc, ga, wc, wa, wo)


def kernel(x, ffn1_norm, ffn1_w_in, ffn1_w_out, mix_norm, w_in, conv_dw, conv_dw_b, conv_ln_g,
           conv_ln_b, conv_w_out, q_norm, k_norm, lam_q, lam_k, attn_subln, attn_w_out, w_out,
           ffn2_norm, ffn2_w_in, ffn2_w_out):
    batch, seq, d = x.shape
    depth = w_in.shape[0]
    assert d == D_MODEL and seq % 512 == 0
    xs = x.reshape(batch * seq, d)
    row1 = lambda p: p.reshape(1, -1)
    for l in range(depth):
        lambda_init = 0.8 - 0.6 * math.exp(-0.3 * l)
        xs = _ffn(xs, row1(ffn1_norm[l]), ffn1_w_in[l].astype(BF16), ffn1_w_out[l].astype(BF16),
                  tm=512)
        qg = jnp.tile(q_norm[l], 2 * N_HEADS).reshape(1, d)
        kg = jnp.tile(k_norm[l], 2 * N_HEADS).reshape(1, d)
        u, q, k, v, gc, ga = _proj(xs, row1(mix_norm[l]), w_in[l].astype(BF16), qg, kg, tm=256)
        conv_act = _conv(u, conv_dw[l], row1(conv_dw_b[l]), row1(conv_ln_g[l]),
                         row1(conv_ln_b[l]), batch=batch, seq=seq, tt=128, rc=32)
        attn_o = _attn(q, k, v, lam_q[l], lam_k[l], row1(attn_subln[l]), batch=batch, seq=seq,
                       tq=512, lambda_init=lambda_init)
        xs = _merge(xs, conv_act, attn_o, gc, ga, conv_w_out[l].astype(BF16),
                    attn_w_out[l].astype(BF16), w_out[l].astype(BF16), tm=512)
        xs = _ffn(xs, row1(ffn2_norm[l]), ffn2_w_in[l].astype(BF16), ffn2_w_out[l].astype(BF16),
                  tm=512)
    return xs.reshape(batch, seq, d)
```

```python
import functools
import math

import jax
import jax.numpy as jnp
from jax import lax
from jax.experimental import pallas as pl
from jax.experimental.pallas import tpu as pltpu

D_MODEL = 1024
N_HEADS = 8
HEAD_DIM = 64
V_DIM = 2 * HEAD_DIM
CONV_K = 31
EPS = 1e-6

VMEM_LIMIT_BYTES = 56 * 1024 * 1024
SUBLANES, LANES = 8, 128
FFN_CHUNK = 256
HALO = 32
ATTN_TQ = 1024
ATTN_TK = 256
ATTN_CW = 256
LOOKAHEAD = 5
Q_SCALE = HEAD_DIM ** -0.5 * math.log2(math.e)

F32 = jnp.float32
BF16 = jnp.bfloat16


def _params(*sem):
    return pltpu.CompilerParams(dimension_semantics=sem, vmem_limit_bytes=VMEM_LIMIT_BYTES)


def _const_spec(shape):
    nd = len(shape)
    return pl.BlockSpec(shape, lambda *_: (0,) * nd, pipeline_mode=pl.Buffered(1))


def _rms_rows(x, g):
    ms = jnp.mean(x * x, axis=-1, keepdims=True)
    return x * lax.rsqrt(ms + EPS) * g


def _dot(a, b):
    return jnp.dot(a, b, preferred_element_type=F32)


def _ffn_kernel(x_ref, g_ref, wa_ref, wb_ref, wo_ref, o_ref, acc_ref, *, d_ff):
    x = x_ref[...]
    h = _rms_rows(x, g_ref[...]).astype(BF16)
    for c in range(d_ff // FFN_CHUNK):
        sl = slice(c * FFN_CHUNK, (c + 1) * FFN_CHUNK)
        a = _dot(h, wa_ref[:, sl])
        b = _dot(h, wb_ref[:, sl])
        gated = (a * jax.nn.sigmoid(a) * b).astype(BF16)
        contrib = _dot(gated, wo_ref[sl, :])
        if c == 0:
            acc_ref[...] = contrib
        else:
            acc_ref[...] += contrib
    o_ref[...] = x + 0.5 * acc_ref[...]


def _ffn(x, g, w_in, w_out, *, tm):
    n, d = x.shape
    d_ff = w_out.shape[0]
    row = pl.BlockSpec((tm, d), lambda i: (i, 0))
    return pl.pallas_call(
        functools.partial(_ffn_kernel, d_ff=d_ff),
        grid=(n // tm,),
        in_specs=[
            row,
            _const_spec((1, d)),
            pl.BlockSpec((d, d_ff), lambda i: (0, 0), pipeline_mode=pl.Buffered(1)),
            pl.BlockSpec((d, d_ff), lambda i: (0, 1), pipeline_mode=pl.Buffered(1)),
            _const_spec((d_ff, d)),
        ],
        out_specs=row,
        out_shape=jax.ShapeDtypeStruct((n, d), F32),
        scratch_shapes=[pltpu.VMEM((tm, d), F32)],
        compiler_params=_params("parallel"),
        name="ffn",
    )(x, g, w_in, w_in, w_out)


def _qk_norm(p, gain):
    outs = []
    for hh in range(N_HEADS):
        slab = p[:, hh * V_DIM:(hh + 1) * V_DIM]
        sq = slab * slab
        lane = lax.broadcasted_iota(jnp.int32, slab.shape, 1)
        first = lane < HEAD_DIM
        s0 = jnp.sum(jnp.where(first, sq, 0.0), axis=-1, keepdims=True)
        s1 = jnp.sum(jnp.where(first, 0.0, sq), axis=-1, keepdims=True)
        inv = jnp.where(first, lax.rsqrt(s0 / HEAD_DIM + EPS), lax.rsqrt(s1 / HEAD_DIM + EPS))
        outs.append(slab * inv * gain[:, hh * V_DIM:(hh + 1) * V_DIM])
    return outs


def _proj_kernel(x_ref, g_ref, w_ref, qg_ref, kg_ref,
                 u_ref, qt_ref, k_ref, vt_ref, gc_ref, ga_ref):
    d = D_MODEL
    h = _rms_rows(x_ref[...], g_ref[...]).astype(BF16)
    a = _dot(h, w_ref[:, 0:d])
    gte = _dot(h, w_ref[:, d:2 * d])
    u_ref[...] = a * jax.nn.sigmoid(gte)
    q = _dot(h, w_ref[:, 2 * d:3 * d])
    for hh, qh in enumerate(_qk_norm(q, qg_ref[...])):
        qt_ref[hh * V_DIM:(hh + 1) * V_DIM, :] = (qh * Q_SCALE).T.astype(BF16)
    k = _dot(h, w_ref[:, 3 * d:4 * d])
    for hh, kh in enumerate(_qk_norm(k, kg_ref[...])):
        k_ref[:, hh * V_DIM:(hh + 1) * V_DIM] = kh.astype(BF16)
    vt_ref[...] = _dot(h, w_ref[:, 4 * d:5 * d]).T.astype(BF16)
    gc_ref[...] = jax.nn.sigmoid(_dot(h, w_ref[:, 5 * d:6 * d]))
    ga_ref[...] = jax.nn.sigmoid(_dot(h, w_ref[:, 6 * d:7 * d]))


def _proj(x, g, w, qg, kg, *, tm):
    n, d = x.shape
    row = pl.BlockSpec((tm, d), lambda i: (i, 0))
    tr = pl.BlockSpec((None, d, tm), lambda i: (i, 0, 0))
    f32o = jax.ShapeDtypeStruct((n, d), F32)
    b16o = jax.ShapeDtypeStruct((n, d), BF16)
    tro = jax.ShapeDtypeStruct((n // tm, d, tm), BF16)
    return pl.pallas_call(
        _proj_kernel,
        grid=(n // tm,),
        in_specs=[row, _const_spec((1, d)), _const_spec(w.shape),
                  _const_spec((1, d)), _const_spec((1, d))],
        out_specs=[row, tr, row, tr, row, row],
        out_shape=[f32o, tro, b16o, tro, f32o, f32o],
        compiler_params=_params("parallel"),
        name="proj",
    )(x, g, w, qg, kg)


def _conv_kernel(u_ref, halo_ref, dw_ref, b_ref, lg_ref, lb_ref, o_ref, win_ref, y_ref, *, tt, rc):
    j = pl.program_id(1)
    win_ref[0:HALO, :] = jnp.where(j > 0, halo_ref[...], 0.0)
    win_ref[HALO:HALO + tt, :] = u_ref[...]
    first = HALO - (CONV_K - 1)
    for c in range(tt // rc):
        r0 = c * rc
        for lt in range(D_MODEL // LANES):
            lanes = slice(lt * LANES, (lt + 1) * LANES)
            acc = jnp.broadcast_to(b_ref[:, lanes], (rc, LANES))
            for phase in range(SUBLANES):
                taps = [(a, SUBLANES * a + phase - first) for a in range(HALO // SUBLANES + 1)
                        if 0 <= SUBLANES * a + phase - first < CONV_K]
                span = rc + SUBLANES * max(a for a, _ in taps)
                if phase == 0:
                    shifted = win_ref[r0:r0 + span, lanes]
                else:
                    block = win_ref[r0:r0 + span + SUBLANES, lanes]
                    shifted = pltpu.roll(block, span + SUBLANES - phase, axis=0)
                for a, tap in taps:
                    acc = acc + dw_ref[tap:tap + 1, lanes] * shifted[SUBLANES * a:SUBLANES * a + rc]
            y_ref[r0:r0 + rc, lanes] = acc
    for c in range(tt // rc):
        rows = slice(c * rc, (c + 1) * rc)
        y = y_ref[rows, :]
        mu = jnp.mean(y, axis=-1, keepdims=True)
        yc = y - mu
        var = jnp.mean(yc * yc, axis=-1, keepdims=True)
        y = yc * lax.rsqrt(var + EPS) * lg_ref[...] + lb_ref[...]
        o_ref[rows, :] = (y * jax.nn.sigmoid(y)).astype(BF16)


def _conv(u, dw, dw_b, ln_g, ln_b, *, batch, seq, tt, rc):
    d = u.shape[-1]
    u3 = u.reshape(batch, seq, d)
    per = tt // HALO
    cur = pl.BlockSpec((None, tt, d), lambda b, j: (b, j, 0))
    halo = pl.BlockSpec((None, HALO, d), lambda b, j: (b, jnp.maximum(j * per - 1, 0), 0))
    vec = pl.BlockSpec((1, d), lambda b, j: (0, 0))
    out = pl.pallas_call(
        functools.partial(_conv_kernel, tt=tt, rc=rc),
        grid=(batch, seq // tt),
        in_specs=[cur, halo, pl.BlockSpec((CONV_K, d), lambda b, j: (0, 0)), vec, vec, vec],
        out_specs=cur,
        out_shape=jax.ShapeDtypeStruct((batch, seq, d), BF16),
        scratch_shapes=[pltpu.VMEM((HALO + tt, d), F32), pltpu.VMEM((tt, d), F32)],
        compiler_params=_params("parallel", "parallel"),
        name="conv",
    )(u3, u3, dw, dw_b, ln_g, ln_b)
    return out.reshape(batch * seq, d)


def _attn_kernel(qt_ref, k_ref, vt_ref, lq_ref, lk_ref, sub_ref, o_ref, *scratch,
                 tq, tk, cw, lambda_init):
    i = pl.program_id(2)
    npiece = tq // cw
    nchain = 2 * npiece
    kp = tk // cw
    ntile = tq // tk
    qs_refs, m_refs, l_refs, acc_refs = (scratch[n * nchain:(n + 1) * nchain] for n in range(4))
    row = lax.broadcasted_iota(jnp.int32, (V_DIM, cw), 0)
    zero = jnp.zeros((V_DIM, cw), BF16)
    for p in range(npiece):
        piece = qt_ref[p]
        qs_refs[p][...] = jnp.where(row < HEAD_DIM, piece, zero)
        qs_refs[npiece + p][...] = jnp.where(row < HEAD_DIM, zero, piece)
    for c in range(nchain):
        m_refs[c][...] = jnp.full((1, cw), -jnp.inf, F32)
        l_refs[c][...] = jnp.zeros((1, cw), F32)
        acc_refs[c][...] = jnp.zeros((V_DIM, cw), F32)

    def scores(step):
        j, c, _ = step
        kt = k_ref[pl.ds(pl.multiple_of(j * tk, tk), tk), :]
        return _dot(kt, qs_refs[c][...])

    def update(step, s):
        j, c, diag = step
        if diag is not None:
            kpos = diag * tk + lax.broadcasted_iota(jnp.int32, s.shape, 0)
            qpos = (c % npiece) * cw + lax.broadcasted_iota(jnp.int32, s.shape, 1)
            s = jnp.where(kpos <= qpos, s, -jnp.inf)
        m_old = m_refs[c][...]
        m_new = jnp.maximum(m_old, jnp.max(s, axis=0, keepdims=True))
        alpha = jnp.exp2(m_old - m_new)
        p = jnp.exp2(s - m_new)
        l_refs[c][...] = alpha * l_refs[c][...] + jnp.sum(p, axis=0, keepdims=True)
        vt = jnp.concatenate([vt_ref[j * kp + t] for t in range(kp)], axis=1)
        acc_refs[c][...] = alpha * acc_refs[c][...] + _dot(vt, p.astype(BF16))
        m_refs[c][...] = m_new

    def run(steps):
        pending = [scores(st) for st in steps[:LOOKAHEAD]]
        for n, st in enumerate(steps):
            s = pending.pop(0)
            update(st, s)
            if n + LOOKAHEAD < len(steps):
                pending.append(scores(steps[n + LOOKAHEAD]))

    def body(it, carry):
        run([(it * ntile + t, c, None) for t in range(ntile) for c in range(nchain)])
        return carry

    lax.fori_loop(0, i, body, 0)
    diag_steps = []
    for t in range(ntile):
        for c in range(nchain):
            q_lo = (c % npiece) * cw
            if t * tk <= q_lo + cw - 1:
                diag_steps.append((i * ntile + t, c, t if (t + 1) * tk - 1 > q_lo else None))
    run(diag_steps)

    lq = lq_ref[...]
    lk = lk_ref[...]
    lam = (jnp.exp(jnp.sum(lq[0:1] * lk[0:1], axis=-1, keepdims=True))
           - jnp.exp(jnp.sum(lq[1:2] * lk[1:2], axis=-1, keepdims=True)) + lambda_init)
    for p in range(npiece):
        o = (acc_refs[p][...] / l_refs[p][...]
             - lam * (acc_refs[npiece + p][...] / l_refs[npiece + p][...]))
        ms = jnp.mean(o * o, axis=0, keepdims=True)
        o = o * lax.rsqrt(ms + EPS) * sub_ref[...] * (1.0 - lambda_init)
        o_ref[p * cw:(p + 1) * cw, :] = o.T.astype(BF16)


def _attn(qt, k, vt, lam_q, lam_k, subln, *, batch, seq, tq, tk, lambda_init):
    d = k.shape[-1]
    cw = qt.shape[-1]
    npc = seq // cw
    nchain = 2 * tq // cw
    qt4 = qt.reshape(batch, npc, d, cw)
    vt4 = vt.reshape(batch, npc, d, cw)
    k3 = k.reshape(batch, seq, d)
    qspec = pl.BlockSpec((None, tq // cw, V_DIM, cw), lambda b, h, i: (b, i, h, 0))
    kspec = pl.BlockSpec((None, seq, V_DIM), lambda b, h, i: (b, 0, h))
    vspec = pl.BlockSpec((None, npc, V_DIM, cw), lambda b, h, i: (b, 0, h, 0))
    ospec = pl.BlockSpec((None, tq, V_DIM), lambda b, h, i: (b, i, h))
    small = lambda shape: pl.BlockSpec(shape, lambda b, h, i: (0, 0))
    out = pl.pallas_call(
        functools.partial(_attn_kernel, tq=tq, tk=tk, cw=cw, lambda_init=lambda_init),
        grid=(batch, N_HEADS, seq // tq),
        in_specs=[qspec, kspec, vspec, small((2, HEAD_DIM)), small((2, HEAD_DIM)),
                  small((V_DIM, 1))],
        out_specs=ospec,
        out_shape=jax.ShapeDtypeStruct((batch, seq, d), BF16),
        scratch_shapes=([pltpu.VMEM((V_DIM, cw), BF16)] * nchain + [pltpu.VMEM((1, cw), F32)] * nchain
                        + [pltpu.VMEM((1, cw), F32)] * nchain
                        + [pltpu.VMEM((V_DIM, cw), F32)] * nchain),
        compiler_params=_params("parallel", "parallel", "arbitrary"),
        name="attn",
    )(qt4, k3, vt4, lam_q, lam_k, subln)
    return out.reshape(batch * seq, d)


def _merge_kernel(x_ref, c_ref, a_ref, gc_ref, ga_ref, wc_ref, wa_ref, wo_ref, o_ref):
    yc = _dot(c_ref[...], wc_ref[...])
    ya = _dot(a_ref[...], wa_ref[...])
    merged = gc_ref[...] * yc + ga_ref[...] * ya
    o_ref[...] = x_ref[...] + _dot(merged.astype(BF16), wo_ref[...])


def _merge(x, conv_act, attn_o, gc, ga, wc, wa, wo, *, tm):
    n, d = x.shape
    row = pl.BlockSpec((tm, d), lambda i: (i, 0))
    w = _const_spec((d, d))
    return pl.pallas_call(
        _merge_kernel,
        grid=(n // tm,),
        in_specs=[row] * 5 + [w, w, w],
        out_specs=row,
        out_shape=jax.ShapeDtypeStruct((n, d), F32),
        compiler_params=_params("parallel"),
        name="merge",
    )(x, conv_act, attn_o, gc, ga, wc, wa, wo)


def kernel(x, ffn1_norm, ffn1_w_in, ffn1_w_out, mix_norm, w_in, conv_dw, conv_dw_b, conv_ln_g,
           conv_ln_b, conv_w_out, q_norm, k_norm, lam_q, lam_k, attn_subln, attn_w_out, w_out,
           ffn2_norm, ffn2_w_in, ffn2_w_out):
    batch, seq, d = x.shape
    depth = w_in.shape[0]
    assert d == D_MODEL and seq % 512 == 0
    xs = x.reshape(batch * seq, d)
    row1 = lambda p: p.reshape(1, -1)
    for l in range(depth):
        lambda_init = 0.8 - 0.6 * math.exp(-0.3 * l)
        xs = _ffn(xs, row1(ffn1_norm[l]), ffn1_w_in[l].astype(BF16), ffn1_w_out[l].astype(BF16),
                  tm=512)
        qg = jnp.tile(q_norm[l], 2 * N_HEADS).reshape(1, d)
        kg = jnp.tile(k_norm[l], 2 * N_HEADS).reshape(1, d)
        u, qt, k, vt, gc, ga = _proj(xs, row1(mix_norm[l]), w_in[l].astype(BF16), qg, kg,
                                     tm=ATTN_CW)
        conv_act = _conv(u, conv_dw[l], row1(conv_dw_b[l]), row1(conv_ln_g[l]),
                         row1(conv_ln_b[l]), batch=batch, seq=seq, tt=256, rc=32)
        attn_o = _attn(qt, k, vt, lam_q[l], lam_k[l], attn_subln[l].reshape(V_DIM, 1),
                       batch=batch, seq=seq, tq=ATTN_TQ, tk=ATTN_TK, lambda_init=lambda_init)
        xs = _merge(xs, conv_act, attn_o, gc, ga, conv_w_out[l].astype(BF16),
                    attn_w_out[l].astype(BF16), w_out[l].astype(BF16), tm=512)
        xs = _ffn(xs, row1(ffn2_norm[l]), ffn2_w_in[l].astype(BF16), ffn2_w_out[l].astype(BF16),
                  tm=512)
    return xs.reshape(batch, seq, d)
```

```python
import functools
import math

import jax
import jax.numpy as jnp
from jax import lax
from jax.experimental import pallas as pl
from jax.experimental.pallas import tpu as pltpu

D_MODEL = 1024
N_HEADS = 8
HEAD_DIM = 64
V_DIM = 2 * HEAD_DIM
CONV_K = 31
EPS = 1e-6

VMEM_LIMIT_BYTES = 56 * 1024 * 1024
SUBLANES, LANES = 8, 128
FFN_CHUNK = 256
HALO = 32
TAIL_TT = 256
CONV_RC = 32
ATTN_TQ = 1024
ATTN_TK = 256
ATTN_CW = 256
LOOKAHEAD = 6
Q_SCALE = HEAD_DIM ** -0.5 * math.log2(math.e)

F32 = jnp.float32
BF16 = jnp.bfloat16


def _params(*sem):
    return pltpu.CompilerParams(dimension_semantics=sem, vmem_limit_bytes=VMEM_LIMIT_BYTES)


def _const_spec(shape):
    nd = len(shape)
    return pl.BlockSpec(shape, lambda *_: (0,) * nd, pipeline_mode=pl.Buffered(1))


def _rms_rows(x, g):
    ms = jnp.mean(x * x, axis=-1, keepdims=True)
    return x * lax.rsqrt(ms + EPS) * g


def _dot(a, b):
    return jnp.dot(a, b, preferred_element_type=F32)


def _ffn_chunk(h, c, wa_ref, wb_ref, wo_ref, acc_ref):
    sl = slice(c * FFN_CHUNK, (c + 1) * FFN_CHUNK)
    a = _dot(h, wa_ref[:, sl])
    b = _dot(h, wb_ref[:, sl])
    gated = (a * jax.nn.sigmoid(a) * b).astype(BF16)
    contrib = _dot(gated, wo_ref[sl, :])
    if c == 0:
        acc_ref[...] = contrib
    else:
        acc_ref[...] += contrib
    return contrib[0:SUBLANES, 0:LANES]


def _ffn_kernel(x_ref, g_ref, wa_ref, wb_ref, wo_ref, o_ref, acc_ref, *, d_ff):
    x = x_ref[...]
    h = _rms_rows(x, g_ref[...]).astype(BF16)
    for c in range(d_ff // FFN_CHUNK):
        _ffn_chunk(h, c, wa_ref, wb_ref, wo_ref, acc_ref)
    o_ref[...] = x + 0.5 * acc_ref[...]


def _ffn(x, g, w_in, w_out, *, tm):
    n, d = x.shape
    d_ff = w_out.shape[0]
    row = pl.BlockSpec((tm, d), lambda i: (i, 0))
    return pl.pallas_call(
        functools.partial(_ffn_kernel, d_ff=d_ff),
        grid=(n // tm,),
        in_specs=[
            row,
            _const_spec((1, d)),
            pl.BlockSpec((d, d_ff), lambda i: (0, 0), pipeline_mode=pl.Buffered(1)),
            pl.BlockSpec((d, d_ff), lambda i: (0, 1), pipeline_mode=pl.Buffered(1)),
            _const_spec((d_ff, d)),
        ],
        out_specs=row,
        out_shape=jax.ShapeDtypeStruct((n, d), F32),
        scratch_shapes=[pltpu.VMEM((tm, d), F32)],
        compiler_params=_params("parallel"),
        name="ffn",
    )(x, g, w_in, w_in, w_out)


def _qk_norm(p, gain):
    outs = []
    for hh in range(N_HEADS):
        slab = p[:, hh * V_DIM:(hh + 1) * V_DIM]
        sq = slab * slab
        lane = lax.broadcasted_iota(jnp.int32, slab.shape, 1)
        first = lane < HEAD_DIM
        s0 = jnp.sum(jnp.where(first, sq, 0.0), axis=-1, keepdims=True)
        s1 = jnp.sum(jnp.where(first, 0.0, sq), axis=-1, keepdims=True)
        inv = jnp.where(first, lax.rsqrt(s0 / HEAD_DIM + EPS), lax.rsqrt(s1 / HEAD_DIM + EPS))
        outs.append(slab * inv * gain[:, hh * V_DIM:(hh + 1) * V_DIM])
    return outs


def _proj_kernel(x_ref, g_ref, w_ref, qg_ref, kg_ref,
                 u_ref, qt_ref, k_ref, vt_ref, gc_ref, ga_ref):
    d = D_MODEL
    h = _rms_rows(x_ref[...], g_ref[...]).astype(BF16)
    a = _dot(h, w_ref[:, 0:d])
    gte = _dot(h, w_ref[:, d:2 * d])
    u_ref[...] = a * jax.nn.sigmoid(gte)
    q = _dot(h, w_ref[:, 2 * d:3 * d])
    for hh, qh in enumerate(_qk_norm(q, qg_ref[...])):
        qt_ref[hh * V_DIM:(hh + 1) * V_DIM, :] = (qh * Q_SCALE).T.astype(BF16)
    k = _dot(h, w_ref[:, 3 * d:4 * d])
    for hh, kh in enumerate(_qk_norm(k, kg_ref[...])):
        k_ref[:, hh * V_DIM:(hh + 1) * V_DIM] = kh.astype(BF16)
    vt_ref[...] = _dot(h, w_ref[:, 4 * d:5 * d]).T.astype(BF16)
    gc_ref[...] = jax.nn.sigmoid(_dot(h, w_ref[:, 5 * d:6 * d]))
    ga_ref[...] = jax.nn.sigmoid(_dot(h, w_ref[:, 6 * d:7 * d]))


def _proj(x, g, w, qg, kg, *, tm):
    n, d = x.shape
    row = pl.BlockSpec((tm, d), lambda i: (i, 0))
    tr = pl.BlockSpec((None, d, tm), lambda i: (i, 0, 0))
    f32o = jax.ShapeDtypeStruct((n, d), F32)
    b16o = jax.ShapeDtypeStruct((n, d), BF16)
    tro = jax.ShapeDtypeStruct((n // tm, d, tm), BF16)
    return pl.pallas_call(
        _proj_kernel,
        grid=(n // tm,),
        in_specs=[row, _const_spec((1, d)), _const_spec(w.shape),
                  _const_spec((1, d)), _const_spec((1, d))],
        out_specs=[row, tr, row, tr, row, row],
        out_shape=[f32o, tro, b16o, tro, f32o, f32o],
        compiler_params=_params("parallel"),
        name="proj",
    )(x, g, w, qg, kg)


def _conv_units(win_ref, dw_ref, b_ref, lg_ref, lb_ref, y_ref, act_ref, *, tt, rc):
    first = HALO - (CONV_K - 1)

    def depthwise(r0, lanes, after=None):
        bias = jnp.broadcast_to(b_ref[:, lanes], (SUBLANES, LANES))
        if after is not None:
            bias = _after(bias, [after])
        acc = jnp.concatenate([bias] * (rc // SUBLANES), axis=0)
        block = win_ref[r0:r0 + rc + HALO, lanes]
        for phase in range(SUBLANES):
            taps = [(a, SUBLANES * a + phase - first) for a in range(HALO // SUBLANES + 1)
                    if 0 <= SUBLANES * a + phase - first < CONV_K]
            shifted = block if phase == 0 else pltpu.roll(block, rc + HALO - phase, axis=0)
            for a, tap in taps:
                acc = acc + dw_ref[tap:tap + 1, lanes] * shifted[SUBLANES * a:SUBLANES * a + rc]
        y_ref[r0:r0 + rc, lanes] = acc
        return acc[0:SUBLANES]

    def norm_act(rows, after=None):
        del after
        y = y_ref[rows, :]
        mu = jnp.mean(y, axis=-1, keepdims=True)
        yc = y - mu
        var = jnp.mean(yc * yc, axis=-1, keepdims=True)
        y = yc * lax.rsqrt(var + EPS) * lg_ref[...] + lb_ref[...]
        y = y * jax.nn.sigmoid(y)
        act_ref[rows, :] = y.astype(BF16)
        return y[0:SUBLANES, 0:LANES]

    units = [functools.partial(depthwise, c * rc, slice(lt * LANES, (lt + 1) * LANES))
             for lt in range(D_MODEL // LANES) for c in range(tt // rc)]
    units += [functools.partial(norm_act, slice(c * rc, (c + 1) * rc)) for c in range(tt // rc)]
    return units


def _zero_bits(values):
    zero = None
    for v in values:
        bits = pltpu.bitcast(v, jnp.uint32)
        z = lax.shift_right_logical(lax.shift_right_logical(bits, jnp.uint32(16)), jnp.uint32(16))
        zero = z if zero is None else zero | z
    return zero


def _after(x, values):
    return pltpu.bitcast(pltpu.bitcast(x, jnp.uint32) + _zero_bits(values), x.dtype)


def _order_after(ref, values):
    if values:
        rows = SUBLANES * 4 // ref.dtype.itemsize
        ref[0:rows, 0:LANES] = _after(ref[0:rows, 0:LANES], values)


def _tail_kernel(u_ref, halo_ref, dw_ref, b_ref, lg_ref, lb_ref,
                 x_ref, a_ref, gc_ref, ga_ref, wc_ref, wa_ref, wo_ref,
                 fg_ref, fwa_ref, fwb_ref, fwo_ref,
                 o_ref, win_ref, y_ref, act_ref, acc_ref, mg_ref, h_ref, gt_ref,
                 *, tt, rc, d_ff, tiles_per_seq):
    s = pl.program_id(0)
    d = D_MODEL

    @pl.when(s == 0)
    def _():
        act_ref[...] = jnp.zeros(act_ref.shape, BF16)

    prev_act = act_ref[...]
    win_ref[0:HALO, :] = jnp.where(s % tiles_per_seq != 0, halo_ref[...], 0.0)
    win_ref[HALO:HALO + tt, :] = u_ref[...]
    conv = _conv_units(win_ref, dw_ref, b_ref, lg_ref, lb_ref, y_ref, act_ref, tt=tt, rc=rc)
    nchunk = d_ff // FFN_CHUNK
    cover = [2 * d * d, d * d] + [2 * d * FFN_CHUNK, FFN_CHUNK * d] * nchunk
    edges = [round(len(conv) * sum(cover[:g]) / sum(cover)) for g in range(len(cover) + 1)]
    groups = iter([conv[lo:hi] for lo, hi in zip(edges[:-1], edges[1:])])
    tile = lambda v: v[0:SUBLANES, 0:LANES]
    run_group = lambda after: [unit(after=after) for unit in next(groups)]

    yc = _dot(prev_act, wc_ref[...])
    ya = _dot(a_ref[...], wa_ref[...])
    mg_ref[...] = (gc_ref[...] * yc + ga_ref[...] * ya).astype(BF16)
    _order_after(mg_ref, run_group(None))
    proj = _dot(mg_ref[...], wo_ref[...])
    x = x_ref[...] + proj
    h_ref[...] = _rms_rows(x, fg_ref[...]).astype(BF16)
    _order_after(h_ref, run_group(tile(ya)))
    last = tile(proj)
    for c in range(nchunk):
        sl = slice(c * FFN_CHUNK, (c + 1) * FFN_CHUNK)
        h = h_ref[...]
        a = _dot(h, fwa_ref[:, sl])
        b = _dot(h, fwb_ref[:, sl])
        gt_ref[...] = (a * jax.nn.sigmoid(a) * b).astype(BF16)
        _order_after(gt_ref, run_group(last))
        contrib = _dot(gt_ref[...], fwo_ref[sl, :])
        if c == 0:
            acc_ref[...] = contrib
        else:
            acc_ref[...] += contrib
        _order_after(acc_ref, run_group(tile(b)))
        last = tile(contrib)
    o_ref[...] = x + 0.5 * acc_ref[...]


def _tail(u, x, attn_o, gc, ga, dw, dw_b, ln_g, ln_b, wc, wa, wo, fg, fw_in, fw_out,
          *, seq, tt, rc):
    n, d = x.shape
    d_ff = fw_out.shape[0]
    nt = n // tt
    per = tt // HALO
    conv_tile = lambda s: jnp.minimum(s, nt - 1)
    cur = pl.BlockSpec((tt, d), lambda s: (conv_tile(s), 0))
    halo = pl.BlockSpec((HALO, d), lambda s: (jnp.maximum(conv_tile(s) * per - 1, 0), 0))
    prev = pl.BlockSpec((tt, d), lambda s: (jnp.maximum(s - 1, 0), 0))
    return pl.pallas_call(
        functools.partial(_tail_kernel, tt=tt, rc=rc, d_ff=d_ff, tiles_per_seq=seq // tt),
        grid=(nt + 1,),
        in_specs=[cur, halo, _const_spec((CONV_K, d)), _const_spec((1, d)), _const_spec((1, d)),
                  _const_spec((1, d)),
                  prev, prev, prev, prev, _const_spec((d, d)), _const_spec((d, d)), _const_spec((d, d)),
                  _const_spec((1, d)),
                  pl.BlockSpec((d, d_ff), lambda s: (0, 0), pipeline_mode=pl.Buffered(1)),
                  pl.BlockSpec((d, d_ff), lambda s: (0, 1), pipeline_mode=pl.Buffered(1)),
                  _const_spec((d_ff, d))],
        out_specs=prev,
        out_shape=jax.ShapeDtypeStruct((n, d), F32),
        scratch_shapes=[pltpu.VMEM((HALO + tt, d), F32), pltpu.VMEM((tt, d), F32),
                        pltpu.VMEM((tt, d), BF16), pltpu.VMEM((tt, d), F32),
                        pltpu.VMEM((tt, d), BF16), pltpu.VMEM((tt, d), BF16),
                        pltpu.VMEM((tt, FFN_CHUNK), BF16)],
        compiler_params=_params("arbitrary"),
        name="tail",
    )(u, u, dw, dw_b, ln_g, ln_b, x, attn_o, gc, ga, wc, wa, wo, fg, fw_in, fw_in, fw_out)


def _attn_kernel(qt_ref, k_ref, vt_ref, lq_ref, lk_ref, sub_ref, o_ref, *scratch,
                 tq, tk, cw, lambda_init):
    i = pl.program_id(2)
    npiece = tq // cw
    nchain = 2 * npiece
    kp = tk // cw
    ntile = tq // tk
    qs_refs, m_refs, l_refs, acc_refs = (scratch[n * nchain:(n + 1) * nchain] for n in range(4))
    row = lax.broadcasted_iota(jnp.int32, (V_DIM, cw), 0)
    zero = jnp.zeros((V_DIM, cw), BF16)
    ones = jnp.ones((2 * SUBLANES, tk), BF16)
    for p in range(npiece):
        piece = qt_ref[p]
        qs_refs[p][...] = jnp.where(row < HEAD_DIM, piece, zero)
        qs_refs[npiece + p][...] = jnp.where(row < HEAD_DIM, zero, piece)
    for c in range(nchain):
        m_refs[c][...] = jnp.full((1, cw), -jnp.inf, F32)
        l_refs[c][...] = jnp.zeros((1, cw), F32)
        acc_refs[c][...] = jnp.zeros((V_DIM, cw), F32)

    def scores(step):
        j, c, _ = step
        kt = k_ref[pl.ds(pl.multiple_of(j * tk, tk), tk), :]
        return _dot(kt, qs_refs[c][...])

    def update(step, s):
        j, c, diag = step
        if diag is not None:
            kpos = diag * tk + lax.broadcasted_iota(jnp.int32, s.shape, 0)
            qpos = (c % npiece) * cw + lax.broadcasted_iota(jnp.int32, s.shape, 1)
            s = jnp.where(kpos <= qpos, s, -jnp.inf)
        m_old = m_refs[c][...]
        m_new = jnp.maximum(m_old, jnp.max(s, axis=0, keepdims=True))
        alpha = jnp.exp2(m_old - m_new)
        p = jnp.exp2(s - m_new).astype(BF16)
        vt = jnp.concatenate([vt_ref[j * kp + t] for t in range(kp)], axis=1)
        pv = _dot(jnp.concatenate([vt, ones], axis=0), p)
        l_refs[c][...] = alpha * l_refs[c][...] + pv[V_DIM:V_DIM + 1]
        acc_refs[c][...] = alpha * acc_refs[c][...] + pv[0:V_DIM]
        m_refs[c][...] = m_new

    def run(steps):
        pending = [scores(st) for st in steps[:LOOKAHEAD]]
        for n, st in enumerate(steps):
            s = pending.pop(0)
            update(st, s)
            if n + LOOKAHEAD < len(steps):
                pending.append(scores(steps[n + LOOKAHEAD]))

    def body(it, carry):
        run([(it * ntile + t, c, None) for t in range(ntile) for c in range(nchain)])
        return carry

    lax.fori_loop(0, i, body, 0)
    diag_steps = []
    for t in range(ntile):
        for c in range(nchain):
            q_lo = (c % npiece) * cw
            if t * tk <= q_lo + cw - 1:
                diag_steps.append((i * ntile + t, c, t if (t + 1) * tk - 1 > q_lo else None))
    run(diag_steps)

    lq = lq_ref[...]
    lk = lk_ref[...]
    lam = (jnp.exp(jnp.sum(lq[0:1] * lk[0:1], axis=-1, keepdims=True))
           - jnp.exp(jnp.sum(lq[1:2] * lk[1:2], axis=-1, keepdims=True)) + lambda_init)
    for p in range(npiece):
        o = (acc_refs[p][...] / l_refs[p][...]
             - lam * (acc_refs[npiece + p][...] / l_refs[npiece + p][...]))
        ms = jnp.mean(o * o, axis=0, keepdims=True)
        o = o * lax.rsqrt(ms + EPS) * sub_ref[...] * (1.0 - lambda_init)
        o_ref[p * cw:(p + 1) * cw, :] = o.T.astype(BF16)


def _attn(qt, k, vt, lam_q, lam_k, subln, *, batch, seq, tq, tk, lambda_init):
    d = k.shape[-1]
    cw = qt.shape[-1]
    npc = seq // cw
    nchain = 2 * tq // cw
    qt4 = qt.reshape(batch, npc, d, cw)
    vt4 = vt.reshape(batch, npc, d, cw)
    k3 = k.reshape(batch, seq, d)
    qspec = pl.BlockSpec((None, tq // cw, V_DIM, cw), lambda b, h, i: (b, i, h, 0))
    kspec = pl.BlockSpec((None, seq, V_DIM), lambda b, h, i: (b, 0, h))
    vspec = pl.BlockSpec((None, npc, V_DIM, cw), lambda b, h, i: (b, 0, h, 0))
    ospec = pl.BlockSpec((None, tq, V_DIM), lambda b, h, i: (b, i, h))
    small = lambda shape: pl.BlockSpec(shape, lambda b, h, i: (0, 0))
    out = pl.pallas_call(
        functools.partial(_attn_kernel, tq=tq, tk=tk, cw=cw, lambda_init=lambda_init),
        grid=(batch, N_HEADS, seq // tq),
        in_specs=[qspec, kspec, vspec, small((2, HEAD_DIM)), small((2, HEAD_DIM)),
                  small((V_DIM, 1))],
        out_specs=ospec,
        out_shape=jax.ShapeDtypeStruct((batch, seq, d), BF16),
        scratch_shapes=([pltpu.VMEM((V_DIM, cw), BF16)] * nchain + [pltpu.VMEM((1, cw), F32)] * nchain
                        + [pltpu.VMEM((1, cw), F32)] * nchain
                        + [pltpu.VMEM((V_DIM, cw), F32)] * nchain),
        compiler_params=_params("parallel", "parallel", "arbitrary"),
        name="attn",
    )(qt4, k3, vt4, lam_q, lam_k, subln)
    return out.reshape(batch * seq, d)


def kernel(x, ffn1_norm, ffn1_w_in, ffn1_w_out, mix_norm, w_in, conv_dw, conv_dw_b, conv_ln_g,
           conv_ln_b, conv_w_out, q_norm, k_norm, lam_q, lam_k, attn_subln, attn_w_out, w_out,
           ffn2_norm, ffn2_w_in, ffn2_w_out):
    batch, seq, d = x.shape
    depth = w_in.shape[0]
    assert d == D_MODEL and seq % 512 == 0
    xs = x.reshape(batch * seq, d)
    row1 = lambda p: p.reshape(1, -1)
    for l in range(depth):
        lambda_init = 0.8 - 0.6 * math.exp(-0.3 * l)
        xs = _ffn(xs, row1(ffn1_norm[l]), ffn1_w_in[l].astype(BF16), ffn1_w_out[l].astype(BF16),
                  tm=512)
        qg = jnp.tile(q_norm[l], 2 * N_HEADS).reshape(1, d)
        kg = jnp.tile(k_norm[l], 2 * N_HEADS).reshape(1, d)
        u, qt, k, vt, gc, ga = _proj(xs, row1(mix_norm[l]), w_in[l].astype(BF16), qg, kg,
                                     tm=ATTN_CW)
        attn_o = _attn(qt, k, vt, lam_q[l], lam_k[l], attn_subln[l].reshape(V_DIM, 1),
                       batch=batch, seq=seq, tq=ATTN_TQ, tk=ATTN_TK, lambda_init=lambda_init)
        xs = _tail(u, xs, attn_o, gc, ga, conv_dw[l], row1(conv_dw_b[l]), row1(conv_ln_g[l]),
                   row1(conv_ln_b[l]), conv_w_out[l].astype(BF16), attn_w_out[l].astype(BF16),
                   w_out[l].astype(BF16), row1(ffn2_norm[l]), ffn2_w_in[l].astype(BF16),
                   ffn2_w_out[l].astype(BF16), seq=seq, tt=TAIL_TT, rc=CONV_RC)
    return xs.reshape(batch, seq, d)
```

```python
import functools
import math

import jax
import jax.numpy as jnp
from jax import lax
from jax.experimental import pallas as pl
from jax.experimental.pallas import tpu as pltpu

D_MODEL = 1024
N_HEADS = 8
HEAD_DIM = 64
V_DIM = 2 * HEAD_DIM
CONV_K = 31
EPS = 1e-6

VMEM_LIMIT_BYTES = 56 * 1024 * 1024
SUBLANES, LANES = 8, 128
FFN_CHUNK = 256
HALO = 32
TAIL_TT = 256
CONV_RC = 64
ATTN_TQ = 2048
ATTN_TK = 256
ATTN_CW = 256
LOOKAHEAD = 6
Q_SCALE =HEAD_DIM ** -0.5 * math.log2(math.e)

F32 = jnp.float32
BF16 = jnp.bfloat16


def _params(*sem):
    return pltpu.CompilerParams(dimension_semantics=sem, vmem_limit_bytes=VMEM_LIMIT_BYTES)


def _const_spec(shape):
    nd = len(shape)
    return pl.BlockSpec(shape, lambda *_: (0,) * nd, pipeline_mode=pl.Buffered(1))


def _rms_rows(x, g):
    ms = jnp.mean(x * x, axis=-1, keepdims=True)
    return x * lax.rsqrt(ms + EPS) * g


def _dot(a, b):
    return jnp.dot(a, b, preferred_element_type=F32)


def _ffn_chunk(h, c, wa_ref, wb_ref, wo_ref, acc_ref):
    sl = slice(c * FFN_CHUNK, (c + 1) * FFN_CHUNK)
    a = _dot(h, wa_ref[:, sl])
    b = _dot(h, wb_ref[:, sl])
    gated = (a * jax.nn.sigmoid(a) * b).astype(BF16)
    contrib = _dot(gated, wo_ref[sl, :])
    if c == 0:
        acc_ref[...] = contrib
    else:
        acc_ref[...] += contrib
    return contrib[0:SUBLANES, 0:LANES]


def _ffn_kernel(x_ref, g_ref, wa_ref, wb_ref, wo_ref, o_ref, acc_ref, *, d_ff):
    x = x_ref[...]
    h = _rms_rows(x, g_ref[...]).astype(BF16)
    for c in range(d_ff // FFN_CHUNK):
        _ffn_chunk(h, c, wa_ref, wb_ref, wo_ref, acc_ref)
    o_ref[...] = x + 0.5 * acc_ref[...]


def _ffn(x, g, w_in, w_out, *, tm):
    n, d = x.shape
    d_ff = w_out.shape[0]
    row = pl.BlockSpec((tm, d), lambda i: (i, 0))
    return pl.pallas_call(
        functools.partial(_ffn_kernel, d_ff=d_ff),
        grid=(n // tm,),
        in_specs=[
            row,
            _const_spec((1, d)),
            pl.BlockSpec((d, d_ff), lambda i: (0, 0), pipeline_mode=pl.Buffered(1)),
            pl.BlockSpec((d, d_ff), lambda i: (0, 1), pipeline_mode=pl.Buffered(1)),
            _const_spec((d_ff, d)),
        ],
        out_specs=row,
        out_shape=jax.ShapeDtypeStruct((n, d), F32),
        scratch_shapes=[pltpu.VMEM((tm, d), F32)],
        compiler_params=_params("parallel"),
        name="ffn",
    )(x, g, w_in, w_in, w_out)


def _qk_norm(p, gain):
    outs = []
    for hh in range(N_HEADS):
        slab = p[:, hh * V_DIM:(hh + 1) * V_DIM]
        sq = slab * slab
        lane = lax.broadcasted_iota(jnp.int32, slab.shape, 1)
        first = lane < HEAD_DIM
        s0 = jnp.sum(jnp.where(first, sq, 0.0), axis=-1, keepdims=True)
        s1 = jnp.sum(jnp.where(first, 0.0, sq), axis=-1, keepdims=True)
        inv = jnp.where(first, lax.rsqrt(s0 / HEAD_DIM + EPS), lax.rsqrt(s1 / HEAD_DIM + EPS))
        outs.append(slab * inv * gain[:, hh * V_DIM:(hh + 1) * V_DIM])
    return outs


def _proj_kernel(x_ref, g_ref, w_ref, qg_ref, kg_ref,
                 u_ref, qt_ref, k_ref, vt_ref, gc_ref, ga_ref):
    d = D_MODEL
    h = _rms_rows(x_ref[...], g_ref[...]).astype(BF16)
    a = _dot(h, w_ref[:, 0:d])
    gte = _dot(h, w_ref[:, d:2 * d])
    u_ref[...] = a * jax.nn.sigmoid(gte)
    q = _dot(h, w_ref[:, 2 * d:3 * d])
    for hh, qh in enumerate(_qk_norm(q, qg_ref[...])):
        qt_ref[hh * V_DIM:(hh + 1) * V_DIM, :] = (qh * Q_SCALE).T.astype(BF16)
    k = _dot(h, w_ref[:, 3 * d:4 * d])
    for hh, kh in enumerate(_qk_norm(k, kg_ref[...])):
        k_ref[:, hh * V_DIM:(hh + 1) * V_DIM] = kh.astype(BF16)
    vt_ref[...] = _dot(h, w_ref[:, 4 * d:5 * d]).T.astype(BF16)
    gc_ref[...] = jax.nn.sigmoid(_dot(h, w_ref[:, 5 * d:6 * d]))
    ga_ref[...] = jax.nn.sigmoid(_dot(h, w_ref[:, 6 * d:7 * d]))


def _proj(x, g, w, qg, kg, *, tm):
    n, d = x.shape
    row = pl.BlockSpec((tm, d), lambda i: (i, 0))
    tr = pl.BlockSpec((None, d, tm), lambda i: (i, 0, 0))
    f32o = jax.ShapeDtypeStruct((n, d), F32)
    b16o = jax.ShapeDtypeStruct((n, d), BF16)
    tro = jax.ShapeDtypeStruct((n // tm, d, tm), BF16)
    return pl.pallas_call(
        _proj_kernel,
        grid=(n // tm,),
        in_specs=[row, _const_spec((1, d)), _const_spec(w.shape),
                  _const_spec((1, d)), _const_spec((1, d))],
        out_specs=[row, tr, row, tr, row, row],
        out_shape=[f32o, tro, b16o, tro, f32o, f32o],
        compiler_params=_params("parallel"),
        name="proj",
    )(x, g, w, qg, kg)


def _conv_units(win_ref, dw_ref, b_ref, lg_ref, lb_ref, y_ref, act_ref, *, tt, rc):
    first = HALO - (CONV_K - 1)

    def depthwise(r0, lanes, after=None):
        bias = jnp.broadcast_to(b_ref[:, lanes], (SUBLANES, LANES))
        if after is not None:
            bias = _after(bias, [after])
        acc = jnp.concatenate([bias] * (rc // SUBLANES), axis=0)
        block = win_ref[r0:r0 + rc + HALO, lanes]
        for phase in range(SUBLANES):
            taps = [(a, SUBLANES * a + phase - first) for a in range(HALO // SUBLANES + 1)
                    if 0 <= SUBLANES * a + phase - first < CONV_K]
            shifted = block if phase == 0 else pltpu.roll(block, rc + HALO - phase, axis=0)
            for a, tap in taps:
                acc = acc + dw_ref[tap:tap + 1, lanes] * shifted[SUBLANES * a:SUBLANES * a + rc]
        y_ref[r0:r0 + rc, lanes] = acc
        return acc[0:SUBLANES]

    def norm_act(rows, after=None):
        del after
        y = y_ref[rows, :]
        mu = jnp.mean(y, axis=-1, keepdims=True)
        yc = y - mu
        var = jnp.mean(yc * yc, axis=-1, keepdims=True)
        y = yc * lax.rsqrt(var + EPS) * lg_ref[...] + lb_ref[...]
        y = y * jax.nn.sigmoid(y)
        act_ref[rows, :] = y.astype(BF16)
        return y[0:SUBLANES, 0:LANES]

    units = [functools.partial(depthwise, c * rc, slice(lt * LANES, (lt + 1) * LANES))
             for lt in range(D_MODEL // LANES) for c in range(tt // rc)]
    units += [functools.partial(norm_act, slice(c * rc, (c + 1) * rc)) for c in range(tt // rc)]
    return units


def _zero_bits(values):
    zero = None
    for v in values:
        bits = pltpu.bitcast(v, jnp.uint32)
        z = lax.shift_right_logical(lax.shift_right_logical(bits, jnp.uint32(16)), jnp.uint32(16))
        zero = z if zero is None else zero | z
    return zero


def _after(x, values):
    return pltpu.bitcast(pltpu.bitcast(x, jnp.uint32) + _zero_bits(values), x.dtype)


def _order_after(ref, values):
    if values:
        rows = SUBLANES * 4 // ref.dtype.itemsize
        ref[0:rows, 0:LANES] = _after(ref[0:rows, 0:LANES], values)


def _tail_kernel(u_ref, halo_ref, dw_ref, b_ref, lg_ref, lb_ref,
                 x_ref, a_ref, gc_ref, ga_ref, wc_ref, wa_ref, wo_ref,
                 fg_ref, fwa_ref, fwb_ref, fwo_ref,
                 o_ref, win_ref, y_ref, act_ref, acc_ref, mg_ref, h_ref, gt_ref,
                 *, tt, rc, d_ff, tiles_per_seq):
    s = pl.program_id(0)
    d = D_MODEL

    @pl.when(s == 0)
    def _():
        act_ref[...] = jnp.zeros(act_ref.shape, BF16)

    prev_act = act_ref[...]
    win_ref[0:HALO, :] = jnp.where(s % tiles_per_seq != 0, halo_ref[...], 0.0)
    win_ref[HALO:HALO + tt, :] = u_ref[...]
    conv = _conv_units(win_ref, dw_ref, b_ref, lg_ref, lb_ref, y_ref, act_ref, tt=tt, rc=rc)
    nchunk = d_ff // FFN_CHUNK
    cover = [2 * d * d, d * d] + [2 * d * FFN_CHUNK, FFN_CHUNK * d] * nchunk
    edges = [round(len(conv) * sum(cover[:g]) / sum(cover)) for g in range(len(cover) + 1)]
    groups = iter([conv[lo:hi] for lo, hi in zip(edges[:-1], edges[1:])])
    tile = lambda v: v[0:SUBLANES, 0:LANES]
    run_group = lambda after: [unit(after=after) for unit in next(groups)]

    yc = _dot(prev_act, wc_ref[...])
    ya = _dot(a_ref[...], wa_ref[...])
    mg_ref[...] = (gc_ref[...] * yc + ga_ref[...] * ya).astype(BF16)
    _order_after(mg_ref, run_group(None))
    proj = _dot(mg_ref[...], wo_ref[...])
    x = x_ref[...] + proj
    h_ref[...] = _rms_rows(x, fg_ref[...]).astype(BF16)
    _order_after(h_ref, run_group(tile(ya)))
    last = tile(proj)
    for c in range(nchunk):
        sl = slice(c * FFN_CHUNK, (c + 1) * FFN_CHUNK)
        h = h_ref[...]
        a = _dot(h, fwa_ref[:, sl])
        b = _dot(h, fwb_ref[:, sl])
        gt_ref[...] = (a * jax.nn.sigmoid(a) * b).astype(BF16)
        _order_after(gt_ref, run_group(last))
        contrib = _dot(gt_ref[...], fwo_ref[sl, :])
        if c == 0:
            acc_ref[...] = contrib
        else:
            acc_ref[...] += contrib
        _order_after(acc_ref, run_group(tile(b)))
        last = tile(contrib)
    o_ref[...] = x + 0.5 * acc_ref[...]


def _tail(u, x, attn_o, gc, ga, dw, dw_b, ln_g, ln_b, wc, wa, wo, fg, fw_in, fw_out,
          *, seq, tt, rc):
    n, d = x.shape
    d_ff = fw_out.shape[0]
    nt = n // tt
    per = tt // HALO
    conv_tile = lambda s: jnp.minimum(s, nt - 1)
    cur = pl.BlockSpec((tt, d), lambda s: (conv_tile(s), 0))
    halo = pl.BlockSpec((HALO, d), lambda s: (jnp.maximum(conv_tile(s) * per - 1, 0), 0))
    prev = pl.BlockSpec((tt, d), lambda s: (jnp.maximum(s - 1, 0), 0))
    return pl.pallas_call(
        functools.partial(_tail_kernel, tt=tt, rc=rc, d_ff=d_ff, tiles_per_seq=seq // tt),
        grid=(nt + 1,),
        in_specs=[cur, halo, _const_spec((CONV_K, d)), _const_spec((1, d)), _const_spec((1, d)),
                  _const_spec((1, d)),
                  prev, prev, prev, prev, _const_spec((d, d)), _const_spec((d, d)), _const_spec((d, d)),
                  _const_spec((1, d)),
                  pl.BlockSpec((d, d_ff), lambda s: (0, 0), pipeline_mode=pl.Buffered(1)),
                  pl.BlockSpec((d, d_ff), lambda s: (0, 1), pipeline_mode=pl.Buffered(1)),
                  _const_spec((d_ff, d))],
        out_specs=prev,
        out_shape=jax.ShapeDtypeStruct((n, d), F32),
        scratch_shapes=[pltpu.VMEM((HALO + tt, d), F32), pltpu.VMEM((tt, d), F32),
                        pltpu.VMEM((tt, d), BF16), pltpu.VMEM((tt, d), F32),
                        pltpu.VMEM((tt, d), BF16), pltpu.VMEM((tt, d), BF16),
                        pltpu.VMEM((tt, FFN_CHUNK), BF16)],
        compiler_params=_params("arbitrary"),
        name="tail",
    )(u, u, dw, dw_b, ln_g, ln_b, x, attn_o, gc, ga, wc, wa, wo, fg, fw_in, fw_in, fw_out)


def _attn_kernel(qt_ref, k_ref, vt_ref, lq_ref, lk_ref, sub_ref, o_ref, *scratch,
                 seq, tq, tk, cw, lambda_init):
    i = pl.program_id(2)
    npiece = tq // cw
    nchain = 2 * npiece
    kp = tk // cw
    ntile = tq // tk
    qs_refs, m_refs, l_refs, acc_refs = (scratch[n * nchain:(n + 1) * nchain] for n in range(4))
    row = lax.broadcasted_iota(jnp.int32, (V_DIM, cw), 0)
    zero = jnp.zeros((V_DIM, cw), BF16)
    ones = jnp.ones((2 * SUBLANES, tk), BF16)
    for p in range(npiece):
        piece = qt_ref[p]
        qs_refs[p][...] = jnp.where(row < HEAD_DIM, piece, zero)
        qs_refs[npiece + p][...] = jnp.where(row < HEAD_DIM, zero, piece)
    for c in range(nchain):
        m_refs[c][...] = jnp.full((1, cw), -jnp.inf, F32)
        l_refs[c][...] = jnp.zeros((1, cw), F32)
        acc_refs[c][...] = jnp.zeros((V_DIM, cw), F32)

    def scores(step):
        j, c, _ = step
        kt = k_ref[pl.ds(pl.multiple_of(j * tk, tk), tk), :]
        return _dot(kt, qs_refs[c][...])

    def update(step, s):
        j, c, diag = step
        if diag is not None:
            kpos = diag * tk + lax.broadcasted_iota(jnp.int32, s.shape, 0)
            qpos = (c % npiece) * cw + lax.broadcasted_iota(jnp.int32, s.shape, 1)
            s = jnp.where(kpos <= qpos, s, -jnp.inf)
        m_old = m_refs[c][...]
        m_new = jnp.maximum(m_old, jnp.max(s, axis=0, keepdims=True))
        alpha = jnp.exp2(m_old - m_new)
        p = jnp.exp2(s - m_new).astype(BF16)
        vt = jnp.concatenate([vt_ref[j * kp + t] for t in range(kp)], axis=1)
        pv = _dot(jnp.concatenate([vt, ones], axis=0), p)
        l_refs[c][...] = alpha * l_refs[c][...] + pv[V_DIM:V_DIM + 1]
        acc_refs[c][...] = alpha * acc_refs[c][...] + pv[0:V_DIM]
        m_refs[c][...] = m_new

    def run(steps):
        pending = [scores(st) for st in steps[:LOOKAHEAD]]
        for n, st in enumerate(steps):
            s = pending.pop(0)
            update(st, s)
            if n + LOOKAHEAD < len(steps):
                pending.append(scores(steps[n + LOOKAHEAD]))

    def body(it, carry):
        run([(it * ntile + t, c, None) for t in range(ntile) for c in range(nchain)])
        return carry

    if seq > tq:
        lax.fori_loop(0, i, body, 0)
    diag_steps = []
    for t in range(ntile):
        for c in range(nchain):
            q_lo = (c % npiece) * cw
            if t * tk <= q_lo + cw - 1:
                diag_steps.append((i * ntile + t, c, t if (t + 1) * tk - 1 > q_lo else None))
    run(diag_steps)

    lq = lq_ref[...]
    lk = lk_ref[...]
    lam = (jnp.exp(jnp.sum(lq[0:1] * lk[0:1], axis=-1, keepdims=True))
           - jnp.exp(jnp.sum(lq[1:2] * lk[1:2], axis=-1, keepdims=True)) + lambda_init)
    for p in range(npiece):
        o = (acc_refs[p][...] / l_refs[p][...]
             - lam * (acc_refs[npiece + p][...] / l_refs[npiece + p][...]))
        ms = jnp.mean(o * o, axis=0, keepdims=True)
        o = o * lax.rsqrt(ms + EPS) * sub_ref[...] * (1.0 - lambda_init)
        o_ref[p * cw:(p + 1) * cw, :] = o.T.astype(BF16)


def _attn(qt, k, vt, lam_q, lam_k, subln, *, batch, seq, tq, tk, lambda_init):
    d = k.shape[-1]
    cw = qt.shape[-1]
    npc = seq // cw
    nchain = 2 * tq // cw
    qt4 = qt.reshape(batch, npc, d, cw)
    vt4 = vt.reshape(batch, npc, d, cw)
    k3 = k.reshape(batch, seq, d)
    qspec = pl.BlockSpec((None, tq // cw, V_DIM, cw), lambda b, h, i: (b, i, h, 0))
    kspec = pl.BlockSpec((None, seq, V_DIM), lambda b, h, i: (b, 0, h))
    vspec = pl.BlockSpec((None, npc, V_DIM, cw), lambda b, h, i: (b, 0, h, 0))
    ospec = pl.BlockSpec((None, tq, V_DIM), lambda b, h, i: (b, i, h))
    small = lambda shape: pl.BlockSpec(shape, lambda b, h, i: (0, 0))
    out = pl.pallas_call(
        functools.partial(_attn_kernel, seq=seq, tq=tq, tk=tk, cw=cw, lambda_init=lambda_init),
        grid=(batch, N_HEADS, seq // tq),
        in_specs=[qspec, kspec, vspec, small((2, HEAD_DIM)), small((2, HEAD_DIM)),
                  small((V_DIM, 1))],
        out_specs=ospec,
        out_shape=jax.ShapeDtypeStruct((batch, seq, d), BF16),
        scratch_shapes=([pltpu.VMEM((V_DIM, cw), BF16)] * nchain + [pltpu.VMEM((1, cw), F32)] * nchain
                        + [pltpu.VMEM((1, cw), F32)] * nchain
                        + [pltpu.VMEM((V_DIM, cw), F32)] * nchain),
        compiler_params=_params("parallel", "parallel", "arbitrary"),
        name="attn",
    )(qt4, k3, vt4, lam_q, lam_k, subln)
    return out.reshape(batch * seq, d)


def kernel(x, ffn1_norm, ffn1_w_in, ffn1_w_out, mix_norm, w_in, conv_dw, conv_dw_b, conv_ln_g,
           conv_ln_b, conv_w_out, q_norm, k_norm, lam_q, lam_k, attn_subln, attn_w_out, w_out,
           ffn2_norm, ffn2_w_in, ffn2_w_out):
    batch, seq, d = x.shape
    depth = w_in.shape[0]
    assert d == D_MODEL and seq % 512 == 0
    xs = x.reshape(batch * seq, d)
    row1 = lambda p: p.reshape(1, -1)
    for l in range(depth):
        lambda_init = 0.8 - 0.6 * math.exp(-0.3 * l)
        xs = _ffn(xs, row1(ffn1_norm[l]), ffn1_w_in[l].astype(BF16), ffn1_w_out[l].astype(BF16),
                  tm=512)
        qg = jnp.tile(q_norm[l], 2 * N_HEADS).reshape(1, d)
        kg = jnp.tile(k_norm[l], 2 * N_HEADS).reshape(1, d)
        u, qt, k, vt, gc, ga = _proj(xs, row1(mix_norm[l]), w_in[l].astype(BF16), qg, kg,
                                     tm=ATTN_CW)
        attn_o = _attn(qt, k, vt, lam_q[l], lam_k[l], attn_subln[l].reshape(V_DIM, 1),
                       batch=batch, seq=seq, tq=ATTN_TQ, tk=ATTN_TK, lambda_init=lambda_init)
        xs = _tail(u, xs, attn_o, gc, ga, conv_dw[l], row1(conv_dw_b[l]), row1(conv_ln_g[l]),
                   row1(conv_ln_b[l]), conv_w_out[l].astype(BF16), attn_w_out[l].astype(BF16),
                   w_out[l].astype(BF16), row1(ffn2_norm[l]), ffn2_w_in[l].astype(BF16),
                   ffn2_w_out[l].astype(BF16), seq=seq, tt=TAIL_TT, rc=CONV_RC)
    return xs.reshape(batch, seq, d)
```

```python
import functools
import math

import jax
import jax.numpy as jnp
from jax import lax
from jax.experimental import pallas as pl
from jax.experimental.pallas import tpu as pltpu

D_MODEL = 1024
N_HEADS = 8
HEAD_DIM = 64
V_DIM = 2 * HEAD_DIM
CONV_K = 31
EPS = 1e-6

VMEM_LIMIT_BYTES = 56 * 1024 * 1024
SUBLANES, LANES = 8, 128
FFN_CHUNK = 256
HALO = 32
FFN_TM = 512
TAIL_TT = 256
CONV_RC = 64
ATTN_TQ = 2048
ATTN_TK = 256
ATTN_CW = 256
LOOKAHEAD = 6
Q_SCALE =HEAD_DIM ** -0.5 * math.log2(math.e)

F32 = jnp.float32
BF16 = jnp.bfloat16


def _params(*sem):
    return pltpu.CompilerParams(dimension_semantics=sem, vmem_limit_bytes=VMEM_LIMIT_BYTES)


def _layer_spec(shape, layer, col=0):
    idx = (layer,) + (0,) * (len(shape) - 1) + (col,)
    return pl.BlockSpec((None,) + tuple(shape), lambda *_: idx, pipeline_mode=pl.Buffered(1))


def _rms_rows(x, g):
    ms = jnp.mean(x * x, axis=-1, keepdims=True)
    return x * lax.rsqrt(ms + EPS) * g


def _dot(a, b):
    return jnp.dot(a, b, preferred_element_type=F32)


def _ffn_chunk(h, c, wa_ref, wb_ref, wo_ref, acc_ref):
    sl = slice(c * FFN_CHUNK, (c + 1) * FFN_CHUNK)
    a = _dot(h, wa_ref[:, sl])
    b = _dot(h, wb_ref[:, sl])
    gated = (a * jax.nn.sigmoid(a) * b).astype(BF16)
    contrib = _dot(gated, wo_ref[sl, :])
    if c == 0:
        acc_ref[...] = contrib
    else:
        acc_ref[...] += contrib
    return contrib[0:SUBLANES, 0:LANES]


def _ffn_kernel(x_ref, g_ref, wa_ref, wb_ref, wo_ref, o_ref, acc_ref, *, d_ff):
    x = x_ref[...]
    h = _rms_rows(x, g_ref[...]).astype(BF16)
    for c in range(d_ff // FFN_CHUNK):
        _ffn_chunk(h, c, wa_ref, wb_ref, wo_ref, acc_ref)
    o_ref[...] = x + 0.5 * acc_ref[...]


def _ffn(x, g, w_in, w_out, *, layer, tm):
    n, d = x.shape
    d_ff = w_out.shape[1]
    row = pl.BlockSpec((tm, d), lambda i: (i, 0))
    return pl.pallas_call(
        functools.partial(_ffn_kernel, d_ff=d_ff),
        grid=(n // tm,),
        in_specs=[
            row,
            _layer_spec((1, d), layer),
            _layer_spec((d, d_ff), layer, 0),
            _layer_spec((d, d_ff), layer, 1),
            _layer_spec((d_ff, d), layer),
        ],
        out_specs=row,
        out_shape=jax.ShapeDtypeStruct((n, d), F32),
        scratch_shapes=[pltpu.VMEM((tm, d), F32)],
        compiler_params=_params("parallel"),
        name="ffn",
    )(x, g, w_in, w_in, w_out)


def _qk_norm(p, gain):
    outs = []
    for hh in range(N_HEADS):
        slab = p[:, hh * V_DIM:(hh + 1) * V_DIM]
        sq = slab * slab
        lane = lax.broadcasted_iota(jnp.int32, slab.shape, 1)
        first = lane < HEAD_DIM
        s0 = jnp.sum(jnp.where(first, sq, 0.0), axis=-1, keepdims=True)
        s1 = jnp.sum(jnp.where(first, 0.0, sq), axis=-1, keepdims=True)
        inv = jnp.where(first, lax.rsqrt(s0 / HEAD_DIM + EPS), lax.rsqrt(s1 / HEAD_DIM + EPS))
        outs.append(slab * inv * gain[:, hh * V_DIM:(hh + 1) * V_DIM])
    return outs


def _proj_kernel(x_ref, g_ref, w_ref, qg_ref, kg_ref,
                 u_ref, qt_ref, k_ref, vt_ref, gc_ref, ga_ref):
    d = D_MODEL
    h = _rms_rows(x_ref[...], g_ref[...]).astype(BF16)
    a = _dot(h, w_ref[:, 0:d])
    gte = _dot(h, w_ref[:, d:2 * d])
    u_ref[...] = a * jax.nn.sigmoid(gte)
    q = _dot(h, w_ref[:, 2 * d:3 * d])
    for hh, qh in enumerate(_qk_norm(q, qg_ref[...])):
        qt_ref[hh * V_DIM:(hh + 1) * V_DIM, :] = (qh * Q_SCALE).T.astype(BF16)
    k = _dot(h, w_ref[:, 3 * d:4 * d])
    for hh, kh in enumerate(_qk_norm(k, kg_ref[...])):
        k_ref[:, hh * V_DIM:(hh + 1) * V_DIM] = kh.astype(BF16)
    vt_ref[...] = _dot(h, w_ref[:, 4 * d:5 * d]).T.astype(BF16)
    gc_ref[...] = jax.nn.sigmoid(_dot(h, w_ref[:, 5 * d:6 * d]))
    ga_ref[...] = jax.nn.sigmoid(_dot(h, w_ref[:, 6 * d:7 * d]))


def _proj(x, g, w, qg, kg, *, layer, tm):
    n, d = x.shape
    row = pl.BlockSpec((tm, d), lambda i: (i, 0))
    tr = pl.BlockSpec((None, d, tm), lambda i: (i, 0, 0))
    f32o = jax.ShapeDtypeStruct((n, d), F32)
    b16o = jax.ShapeDtypeStruct((n, d), BF16)
    tro = jax.ShapeDtypeStruct((n // tm, d, tm), BF16)
    return pl.pallas_call(
        _proj_kernel,
        grid=(n // tm,),
        in_specs=[row, _layer_spec((1, d), layer), _layer_spec(w.shape[1:], layer),
                  _layer_spec((1, d), layer), _layer_spec((1, d), layer)],
        out_specs=[row, tr, row, tr, row, row],
        out_shape=[f32o, tro, b16o, tro, f32o, f32o],
        compiler_params=_params("parallel"),
        name="proj",
    )(x, g, w, qg, kg)


def _conv_units(win_ref, wsh_ref, dw_ref, b_ref, lg_ref, lb_ref, y_ref, act_ref, *, tt, rc):
    first = HALO - (CONV_K - 1)
    rows = HALO + tt

    def shift(lanes, phase, after=None):
        del after
        moved = pltpu.roll(win_ref[:, lanes], rows - phase, axis=0)
        wsh_ref[phase - 1, :, lanes] = moved
        return moved[0:SUBLANES]

    def depthwise(r0, lanes, after=None):
        bias = jnp.broadcast_to(b_ref[:, lanes], (SUBLANES, LANES))
        if after is not None:
            bias = _after(bias, [after])
        acc = jnp.concatenate([bias] * (rc // SUBLANES), axis=0)
        for tap in range(CONV_K):
            a, phase = divmod(first + tap, SUBLANES)
            lo = r0 + SUBLANES * a
            src = win_ref[lo:lo + rc, lanes] if phase == 0 else wsh_ref[phase - 1, lo:lo + rc, lanes]
            acc = acc + dw_ref[tap:tap + 1, lanes] * src
        y_ref[r0:r0 + rc, lanes] = acc
        return acc[0:SUBLANES]

    def norm_act(rows, after=None):
        del after
        y = y_ref[rows, :]
        mu = jnp.mean(y, axis=-1, keepdims=True)
        yc = y - mu
        var = jnp.mean(yc * yc, axis=-1, keepdims=True)
        y = yc * lax.rsqrt(var + EPS) * lg_ref[...] + lb_ref[...]
        y = y * jax.nn.sigmoid(y)
        act_ref[rows, :] = y.astype(BF16)
        return y[0:SUBLANES, 0:LANES]

    units = []
    for lt in range(D_MODEL // LANES):
        lanes = slice(lt * LANES, (lt + 1) * LANES)
        units += [functools.partial(shift, lanes, p) for p in range(1, SUBLANES)]
        units += [functools.partial(depthwise, c * rc, lanes) for c in range(tt // rc)]
    units += [functools.partial(norm_act, slice(c * rc, (c + 1) * rc)) for c in range(tt // rc)]
    return units


def _zero_bits(values):
    zero = None
    for v in values:
        bits = pltpu.bitcast(v, jnp.uint32)
        z = lax.shift_right_logical(lax.shift_right_logical(bits, jnp.uint32(16)), jnp.uint32(16))
        zero = z if zero is None else zero | z
    return zero


def _after(x, values):
    return pltpu.bitcast(pltpu.bitcast(x, jnp.uint32) + _zero_bits(values), x.dtype)


def _order_after(ref, values):
    if values:
        rows = SUBLANES * 4 // ref.dtype.itemsize
        ref[0:rows, 0:LANES] = _after(ref[0:rows, 0:LANES], values)


def _tail_kernel(u_ref, halo_ref, dw_ref, b_ref, lg_ref, lb_ref,
                 x_ref, a_ref, gc_ref, ga_ref, wc_ref, wa_ref, wo_ref,
                 fg_ref, fwa_ref, fwb_ref, fwo_ref,
                 o_ref, win_ref, wsh_ref, y_ref, act_ref, acc_ref, mg_ref, h_ref, gt_ref,
                 *, tt, rc, d_ff, tiles_per_seq):
    s = pl.program_id(0)
    d = D_MODEL

    @pl.when(s == 0)
    def _():
        act_ref[...] = jnp.zeros(act_ref.shape, BF16)

    prev_act = act_ref[...]
    win_ref[0:HALO, :] = jnp.where(s % tiles_per_seq != 0, halo_ref[...], 0.0)
    win_ref[HALO:HALO + tt, :] = u_ref[...]
    conv = _conv_units(win_ref, wsh_ref, dw_ref, b_ref, lg_ref, lb_ref, y_ref, act_ref, tt=tt, rc=rc)
    nchunk = d_ff // FFN_CHUNK
    cover = [2 * d * d, d * d] + [2 * d * FFN_CHUNK, FFN_CHUNK * d] * nchunk
    edges = [round(len(conv) * sum(cover[:g]) / sum(cover)) for g in range(len(cover) + 1)]
    groups = iter([conv[lo:hi] for lo, hi in zip(edges[:-1], edges[1:])])
    tile = lambda v: v[0:SUBLANES, 0:LANES]
    run_group = lambda after: [unit(after=after) for unit in next(groups)]

    yc = _dot(prev_act, wc_ref[...])
    ya = _dot(a_ref[...], wa_ref[...])
    mg_ref[...] = (gc_ref[...] * yc + ga_ref[...] * ya).astype(BF16)
    _order_after(mg_ref, run_group(None))
    proj = _dot(mg_ref[...], wo_ref[...])
    x = x_ref[...] + proj
    h_ref[...] = _rms_rows(x, fg_ref[...]).astype(BF16)
    _order_after(h_ref, run_group(tile(ya)))
    last = tile(proj)
    for c in range(nchunk):
        sl = slice(c * FFN_CHUNK, (c + 1) * FFN_CHUNK)
        h = h_ref[...]
        a = _dot(h, fwa_ref[:, sl])
        b = _dot(h, fwb_ref[:, sl])
        gt_ref[...] = (a * jax.nn.sigmoid(a) * b).astype(BF16)
        _order_after(gt_ref, run_group(last))
        contrib = _dot(gt_ref[...], fwo_ref[sl, :])
        if c == 0:
            acc_ref[...] = contrib
        else:
            acc_ref[...] += contrib
        _order_after(acc_ref, run_group(tile(b)))
        last = tile(contrib)
    o_ref[...] = x + 0.5 * acc_ref[...]


def _tail(u, x, attn_o, gc, ga, dw, dw_b, ln_g, ln_b, wc, wa, wo, fg, fw_in, fw_out,
          *, layer, seq, tt, rc):
    n, d = x.shape
    d_ff = fw_out.shape[1]
    vec = _layer_spec((1, d), layer)
    mat = _layer_spec((d, d), layer)
    nt = n // tt
    per = tt // HALO
    conv_tile = lambda s: jnp.minimum(s, nt - 1)
    cur = pl.BlockSpec((tt, d), lambda s: (conv_tile(s), 0))
    halo = pl.BlockSpec((HALO, d), lambda s: (jnp.maximum(conv_tile(s) * per - 1, 0), 0))
    prev = pl.BlockSpec((tt, d), lambda s: (jnp.maximum(s - 1, 0), 0))
    return pl.pallas_call(
        functools.partial(_tail_kernel, tt=tt, rc=rc, d_ff=d_ff, tiles_per_seq=seq // tt),
        grid=(nt + 1,),
        in_specs=[cur, halo, _layer_spec((CONV_K, d), layer), vec, vec, vec,
                  prev, prev, prev, prev, mat, mat, mat,
                  vec, _layer_spec((d, d_ff), layer, 0), _layer_spec((d, d_ff), layer, 1),
                  _layer_spec((d_ff, d), layer)],
        out_specs=prev,
        out_shape=jax.ShapeDtypeStruct((n, d), F32),
        scratch_shapes=[pltpu.VMEM((HALO + tt, d), F32),
                        pltpu.VMEM((SUBLANES - 1, HALO + tt, d), F32), pltpu.VMEM((tt, d), F32),
                        pltpu.VMEM((tt, d), BF16), pltpu.VMEM((tt, d), F32),
                        pltpu.VMEM((tt, d), BF16), pltpu.VMEM((tt, d), BF16),
                        pltpu.VMEM((tt, FFN_CHUNK), BF16)],
        compiler_params=_params("arbitrary"),
        name="tail",
    )(u, u, dw, dw_b, ln_g, ln_b, x, attn_o, gc, ga, wc, wa, wo, fg, fw_in, fw_in, fw_out)


def _attn_kernel(qt_ref, k_ref, vt_ref, lq_ref, lk_ref, sub_ref, o_ref, *scratch,
                 seq, tq, tk, cw, lambda_init):
    i = pl.program_id(2)
    npiece = tq // cw
    nchain = 2 * npiece
    kp = tk // cw
    ntile = tq // tk
    qs_refs, m_refs, l_refs, acc_refs = (scratch[n * nchain:(n + 1) * nchain] for n in range(4))
    row = lax.broadcasted_iota(jnp.int32, (V_DIM, cw), 0)
    zero = jnp.zeros((V_DIM, cw), BF16)
    ones = jnp.ones((2 * SUBLANES, tk), BF16)
    for p in range(npiece):
        piece = qt_ref[p]
        qs_refs[p][...] = jnp.where(row < HEAD_DIM, piece, zero)
        qs_refs[npiece + p][...] = jnp.where(row < HEAD_DIM, zero, piece)
    for c in range(nchain):
        m_refs[c][...] = jnp.full((1, cw), -jnp.inf, F32)
        l_refs[c][...] = jnp.zeros((1, cw), F32)
        acc_refs[c][...] = jnp.zeros((V_DIM, cw), F32)

    def scores(step):
        j, c, _ = step
        kt = k_ref[pl.ds(pl.multiple_of(j * tk, tk), tk), :]
        return _dot(kt, qs_refs[c][...])

    def update(step, s):
        j, c, diag = step
        if diag is not None:
            kpos = diag * tk + lax.broadcasted_iota(jnp.int32, s.shape, 0)
            qpos = (c % npiece) * cw + lax.broadcasted_iota(jnp.int32, s.shape, 1)
            s = jnp.where(kpos <= qpos, s, -jnp.inf)
        m_old = m_refs[c][...]
        m_new = jnp.maximum(m_old, jnp.max(s, axis=0, keepdims=True))
        alpha = jnp.exp2(m_old - m_new)
        p = jnp.exp2(s - m_new).astype(BF16)
        vt = jnp.concatenate([vt_ref[j * kp + t] for t in range(kp)], axis=1)
        pv = _dot(jnp.concatenate([vt, ones], axis=0), p)
        l_refs[c][...] = alpha * l_refs[c][...] + pv[V_DIM:V_DIM + 1]
        acc_refs[c][...] = alpha * acc_refs[c][...] + pv[0:V_DIM]
        m_refs[c][...] = m_new

    def run(steps):
        pending = [scores(st) for st in steps[:LOOKAHEAD]]
        for n, st in enumerate(steps):
            s = pending.pop(0)
            update(st, s)
            if n + LOOKAHEAD < len(steps):
                pending.append(scores(steps[n + LOOKAHEAD]))

    def body(it, carry):
        run([(it * ntile + t, c, None) for t in range(ntile) for c in range(nchain)])
        return carry

    if seq > tq:
        lax.fori_loop(0, i, body, 0)
    diag_steps = []
    for t in range(ntile):
        for c in range(nchain):
            q_lo = (c % npiece) * cw
            if t * tk <= q_lo + cw - 1:
                diag_steps.append((i * ntile + t, c, t if (t + 1) * tk - 1 > q_lo else None))
    run(diag_steps)

    lq = lq_ref[...]
    lk = lk_ref[...]
    lam = (jnp.exp(jnp.sum(lq[0:1] * lk[0:1], axis=-1, keepdims=True))
           - jnp.exp(jnp.sum(lq[1:2] * lk[1:2], axis=-1, keepdims=True)) + lambda_init)
    for p in range(npiece):
        o = (acc_refs[p][...] / l_refs[p][...]
             - lam * (acc_refs[npiece + p][...] / l_refs[npiece + p][...]))
        ms = jnp.mean(o * o, axis=0, keepdims=True)
        o = o * lax.rsqrt(ms + EPS) * sub_ref[...] * (1.0 - lambda_init)
        o_ref[p * cw:(p + 1) * cw, :] = o.T.astype(BF16)


def _attn(qt, k, vt, lam_q, lam_k, subln, *, layer, batch, seq, tq, tk, lambda_init):
    d = k.shape[-1]
    cw = qt.shape[-1]
    npc = seq // cw
    nchain = 2 * tq // cw
    qt4 = qt.reshape(batch, npc, d, cw)
    vt4 = vt.reshape(batch, npc, d, cw)
    k3 = k.reshape(batch, seq, d)
    qspec = pl.BlockSpec((None, tq // cw, V_DIM, cw), lambda b, h, i: (b, i, h, 0))
    kspec = pl.BlockSpec((None, seq, V_DIM), lambda b, h, i: (b, 0, h))
    vspec = pl.BlockSpec((None, npc, V_DIM, cw), lambda b, h, i: (b, 0, h, 0))
    ospec = pl.BlockSpec((None, tq, V_DIM), lambda b, h, i: (b, i, h))
    small = lambda shape: pl.BlockSpec((None,) + shape, lambda b, h, i: (layer, 0, 0))
    out = pl.pallas_call(
        functools.partial(_attn_kernel, seq=seq, tq=tq, tk=tk, cw=cw, lambda_init=lambda_init),
        grid=(batch, N_HEADS, seq // tq),
        in_specs=[qspec, kspec, vspec, small((2, HEAD_DIM)), small((2, HEAD_DIM)),
                  small((V_DIM, 1))],
        out_specs=ospec,
        out_shape=jax.ShapeDtypeStruct((batch, seq, d), BF16),
        scratch_shapes=([pltpu.VMEM((V_DIM, cw), BF16)] * nchain + [pltpu.VMEM((1, cw), F32)] * nchain
                        + [pltpu.VMEM((1, cw), F32)] * nchain
                        + [pltpu.VMEM((V_DIM, cw), F32)] * nchain),
        compiler_params=_params("parallel", "parallel", "arbitrary"),
        name="attn",
    )(qt4, k3, vt4, lam_q, lam_k, subln)
    return out.reshape(batch * seq, d)


def kernel(x, ffn1_norm, ffn1_w_in, ffn1_w_out, mix_norm, w_in, conv_dw, conv_dw_b, conv_ln_g,
           conv_ln_b, conv_w_out, q_norm, k_norm, lam_q, lam_k, attn_subln, attn_w_out, w_out,
           ffn2_norm, ffn2_w_in, ffn2_w_out):
    batch, seq, d = x.shape
    depth = w_in.shape[0]
    assert d == D_MODEL and seq % 512 == 0
    xs = x.reshape(batch * seq, d)
    rows = lambda p: p.reshape(depth, 1, -1)
    b16 = lambda p: p.astype(BF16)
    ffn1_w_in, ffn1_w_out, w_in = b16(ffn1_w_in), b16(ffn1_w_out), b16(w_in)
    conv_w_out, attn_w_out, w_out = b16(conv_w_out), b16(attn_w_out), b16(w_out)
    ffn2_w_in, ffn2_w_out = b16(ffn2_w_in), b16(ffn2_w_out)
    qg = rows(jnp.tile(q_norm, (1, 2 * N_HEADS)))
    kg = rows(jnp.tile(k_norm, (1, 2 * N_HEADS)))
    subln = attn_subln.reshape(depth, V_DIM, 1)
    for l in range(depth):
        lambda_init = 0.8 - 0.6 * math.exp(-0.3 * l)
        xs = _ffn(xs, rows(ffn1_norm), ffn1_w_in, ffn1_w_out, layer=l, tm=FFN_TM)
        u, qt, k, vt, gc, ga = _proj(xs, rows(mix_norm), w_in, qg, kg, layer=l, tm=ATTN_CW)
        attn_o = _attn(qt, k, vt, lam_q, lam_k, subln, layer=l, batch=batch, seq=seq,
                       tq=ATTN_TQ, tk=ATTN_TK, lambda_init=lambda_init)
        xs = _tail(u, xs, attn_o, gc, ga, conv_dw, rows(conv_dw_b), rows(conv_ln_g),
                   rows(conv_ln_b), conv_w_out, attn_w_out, w_out, rows(ffn2_norm), ffn2_w_in,
                   ffn2_w_out, layer=l, seq=seq, tt=TAIL_TT, rc=CONV_RC)
    return xs.reshape(batch, seq, d)
```

```python
import functools
import math

import jax
import jax.numpy as jnp
from jax import lax
from jax.experimental import pallas as pl
from jax.experimental.pallas import tpu as pltpu

D_MODEL = 1024
N_HEADS = 8
HEAD_DIM = 64
V_DIM = 2 * HEAD_DIM
CONV_K = 31
EPS = 1e-6

VMEM_LIMIT_BYTES = 56 * 1024 * 1024
SUBLANES, LANES = 8, 128
FFN_CHUNK = 256
HALO = 32
FFN_TM = 512
TAIL_TT = 256
CONV_RC = 64
ATTN_TQ = 2048
ATTN_TK = 256
ATTN_CW = 256
LOOKAHEAD = 6
CHAIN_GAP = 8
Q_SCALE =HEAD_DIM ** -0.5 * math.log2(math.e)

F32 = jnp.float32
BF16 = jnp.bfloat16


def _params(*sem):
    return pltpu.CompilerParams(dimension_semantics=sem, vmem_limit_bytes=VMEM_LIMIT_BYTES)


def _layer_spec(shape, layer, col=0):
    idx = (layer,) + (0,) * (len(shape) - 1) + (col,)
    return pl.BlockSpec((None,) + tuple(shape), lambda *_: idx, pipeline_mode=pl.Buffered(1))


def _rms_rows(x, g):
    ms = jnp.mean(x * x, axis=-1, keepdims=True)
    return x * lax.rsqrt(ms + EPS) * g


def _dot(a, b):
    return jnp.dot(a, b, preferred_element_type=F32)


def _ffn_chunk(h, c, wa_ref, wb_ref, wo_ref, acc_ref):
    sl = slice(c * FFN_CHUNK, (c + 1) * FFN_CHUNK)
    a = _dot(h, wa_ref[:, sl])
    b = _dot(h, wb_ref[:, sl])
    gated = (a * jax.nn.sigmoid(a) * b).astype(BF16)
    contrib = _dot(gated, wo_ref[sl, :])
    if c == 0:
        acc_ref[...] = contrib
    else:
        acc_ref[...] += contrib
    return contrib[0:SUBLANES, 0:LANES]


def _ffn_kernel(x_ref, g_ref, wa_ref, wb_ref, wo_ref, o_ref, acc_ref, *, d_ff):
    x = x_ref[...]
    h = _rms_rows(x, g_ref[...]).astype(BF16)
    for c in range(d_ff // FFN_CHUNK):
        _ffn_chunk(h, c, wa_ref, wb_ref, wo_ref, acc_ref)
    o_ref[...] = x + 0.5 * acc_ref[...]


def _ffn(x, g, w_in, w_out, *, layer, tm):
    n, d = x.shape
    d_ff = w_out.shape[1]
    row = pl.BlockSpec((tm, d), lambda i: (i, 0))
    return pl.pallas_call(
        functools.partial(_ffn_kernel, d_ff=d_ff),
        grid=(n // tm,),
        in_specs=[
            row,
            _layer_spec((1, d), layer),
            _layer_spec((d, d_ff), layer, 0),
            _layer_spec((d, d_ff), layer, 1),
            _layer_spec((d_ff, d), layer),
        ],
        out_specs=row,
        out_shape=jax.ShapeDtypeStruct((n, d), F32),
        scratch_shapes=[pltpu.VMEM((tm, d), F32)],
        compiler_params=_params("parallel"),
        name="ffn",
    )(x, g, w_in, w_in, w_out)


def _qk_norm(p, gain):
    outs = []
    for hh in range(N_HEADS):
        slab = p[:, hh * V_DIM:(hh + 1) * V_DIM]
        sq = slab * slab
        lane = lax.broadcasted_iota(jnp.int32, slab.shape, 1)
        first = lane < HEAD_DIM
        s0 = jnp.sum(jnp.where(first, sq, 0.0), axis=-1, keepdims=True)
        s1 = jnp.sum(jnp.where(first, 0.0, sq), axis=-1, keepdims=True)
        inv = jnp.where(first, lax.rsqrt(s0 / HEAD_DIM + EPS), lax.rsqrt(s1 / HEAD_DIM + EPS))
        outs.append(slab * inv * gain[:, hh * V_DIM:(hh + 1) * V_DIM])
    return outs


def _proj_kernel(x_ref, g_ref, w_ref, qg_ref, kg_ref,
                 u_ref, qt_ref, k_ref, vt_ref, gc_ref, ga_ref):
    d = D_MODEL
    h = _rms_rows(x_ref[...], g_ref[...]).astype(BF16)
    a = _dot(h, w_ref[:, 0:d])
    gte = _dot(h, w_ref[:, d:2 * d])
    u_ref[...] = a * jax.nn.sigmoid(gte)
    q = _dot(h, w_ref[:, 2 * d:3 * d])
    for hh, qh in enumerate(_qk_norm(q, qg_ref[...])):
        qt_ref[hh * V_DIM:(hh + 1) * V_DIM, :] = (qh * Q_SCALE).T.astype(BF16)
    k = _dot(h, w_ref[:, 3 * d:4 * d])
    for hh, kh in enumerate(_qk_norm(k, kg_ref[...])):
        k_ref[:, hh * V_DIM:(hh + 1) * V_DIM] = kh.astype(BF16)
    vt_ref[...] = _dot(h, w_ref[:, 4 * d:5 * d]).T.astype(BF16)
    gc_ref[...] = jax.nn.sigmoid(_dot(h, w_ref[:, 5 * d:6 * d]))
    ga_ref[...] = jax.nn.sigmoid(_dot(h, w_ref[:, 6 * d:7 * d]))


def _proj(x, g, w, qg, kg, *, layer, tm):
    n, d = x.shape
    row = pl.BlockSpec((tm, d), lambda i: (i, 0))
    tr = pl.BlockSpec((None, d, tm), lambda i: (i, 0, 0))
    f32o = jax.ShapeDtypeStruct((n, d), F32)
    b16o = jax.ShapeDtypeStruct((n, d), BF16)
    tro = jax.ShapeDtypeStruct((n // tm, d, tm), BF16)
    return pl.pallas_call(
        _proj_kernel,
        grid=(n // tm,),
        in_specs=[row, _layer_spec((1, d), layer), _layer_spec(w.shape[1:], layer),
                  _layer_spec((1, d), layer), _layer_spec((1, d), layer)],
        out_specs=[row, tr, row, tr, row, row],
        out_shape=[f32o, tro, b16o, tro, f32o, f32o],
        compiler_params=_params("parallel"),
        name="proj",
    )(x, g, w, qg, kg)


def _conv_units(win_ref, wsh_ref, dw_ref, b_ref, lg_ref, lb_ref, y_ref, act_ref, *, tt, rc):
    first = HALO - (CONV_K - 1)
    rows = HALO + tt

    def shift(lanes, phase, after=None):
        del after
        moved = pltpu.roll(win_ref[:, lanes], rows - phase, axis=0)
        wsh_ref[phase - 1, :, lanes] = moved
        return moved[0:SUBLANES]

    def depthwise(r0, lanes, after=None):
        bias = jnp.broadcast_to(b_ref[:, lanes], (SUBLANES, LANES))
        if after is not None:
            bias = _after(bias, [after])
        acc = jnp.concatenate([bias] * (rc // SUBLANES), axis=0)
        for tap in range(CONV_K):
            a, phase = divmod(first + tap, SUBLANES)
            lo = r0 + SUBLANES * a
            src = win_ref[lo:lo + rc, lanes] if phase == 0 else wsh_ref[phase - 1, lo:lo + rc, lanes]
            acc = acc + dw_ref[tap:tap + 1, lanes] * src
        y_ref[r0:r0 + rc, lanes] = acc
        return acc[0:SUBLANES]

    def norm_act(rows, after=None):
        del after
        y = y_ref[rows, :]
        mu = jnp.mean(y, axis=-1, keepdims=True)
        yc = y - mu
        var = jnp.mean(yc * yc, axis=-1, keepdims=True)
        y = yc * lax.rsqrt(var + EPS) * lg_ref[...] + lb_ref[...]
        y = y * jax.nn.sigmoid(y)
        act_ref[rows, :] = y.astype(BF16)
        return y[0:SUBLANES, 0:LANES]

    units = []
    for lt in range(D_MODEL // LANES):
        lanes = slice(lt * LANES, (lt + 1) * LANES)
        units += [functools.partial(shift, lanes, p) for p in range(1, SUBLANES)]
        units += [functools.partial(depthwise, c * rc, lanes) for c in range(tt // rc)]
    units += [functools.partial(norm_act, slice(c * rc, (c + 1) * rc)) for c in range(tt // rc)]
    return units


def _zero_bits(values):
    zero = None
    for v in values:
        bits = pltpu.bitcast(v, jnp.uint32)
        z = lax.shift_right_logical(lax.shift_right_logical(bits, jnp.uint32(16)), jnp.uint32(16))
        zero = z if zero is None else zero | z
    return zero


def _after(x, values):
    return pltpu.bitcast(pltpu.bitcast(x, jnp.uint32) + _zero_bits(values), x.dtype)


def _order_after(ref, values):
    if values:
        rows = SUBLANES * 4 // ref.dtype.itemsize
        ref[0:rows, 0:LANES] = _after(ref[0:rows, 0:LANES], values)


def _tail_kernel(u_ref, halo_ref, dw_ref, b_ref, lg_ref, lb_ref,
                 x_ref, a_ref, gc_ref, ga_ref, wc_ref, wa_ref, wo_ref,
                 fg_ref, fwa_ref, fwb_ref, fwo_ref,
                 o_ref, win_ref, wsh_ref, y_ref, act_ref, acc_ref, mg_ref, h_ref, gt_ref,
                 *, tt, rc, d_ff, tiles_per_seq):
    s = pl.program_id(0)
    d = D_MODEL

    @pl.when(s == 0)
    def _():
        act_ref[...] = jnp.zeros(act_ref.shape, BF16)

    prev_act = act_ref[...]
    win_ref[0:HALO, :] = jnp.where(s % tiles_per_seq != 0, halo_ref[...], 0.0)
    win_ref[HALO:HALO + tt, :] = u_ref[...]
    conv = _conv_units(win_ref, wsh_ref, dw_ref, b_ref, lg_ref, lb_ref, y_ref, act_ref, tt=tt, rc=rc)
    nchunk = d_ff // FFN_CHUNK
    cover = [2 * d * d, d * d] + [2 * d * FFN_CHUNK, FFN_CHUNK * d] * nchunk
    edges = [round(len(conv) * sum(cover[:g]) / sum(cover)) for g in range(len(cover) + 1)]
    groups = iter([conv[lo:hi] for lo, hi in zip(edges[:-1], edges[1:])])
    tile = lambda v: v[0:SUBLANES, 0:LANES]
    run_group = lambda after: [unit(after=after) for unit in next(groups)]

    yc = _dot(prev_act, wc_ref[...])
    ya = _dot(a_ref[...], wa_ref[...])
    mg_ref[...] = (gc_ref[...] * yc + ga_ref[...] * ya).astype(BF16)
    _order_after(mg_ref, run_group(None))
    proj = _dot(mg_ref[...], wo_ref[...])
    x = x_ref[...] + proj
    h_ref[...] = _rms_rows(x, fg_ref[...]).astype(BF16)
    _order_after(h_ref, run_group(tile(ya)))
    last = tile(proj)
    for c in range(nchunk):
        sl = slice(c * FFN_CHUNK, (c + 1) * FFN_CHUNK)
        h = h_ref[...]
        a = _dot(h, fwa_ref[:, sl])
        b = _dot(h, fwb_ref[:, sl])
        gt_ref[...] = (a * jax.nn.sigmoid(a) * b).astype(BF16)
        _order_after(gt_ref, run_group(last))
        contrib = _dot(gt_ref[...], fwo_ref[sl, :])
        if c == 0:
            acc_ref[...] = contrib
        else:
            acc_ref[...] += contrib
        _order_after(acc_ref, run_group(tile(b)))
        last = tile(contrib)
    o_ref[...] = x + 0.5 * acc_ref[...]


def _tail(u, x, attn_o, gc, ga, dw, dw_b, ln_g, ln_b, wc, wa, wo, fg, fw_in, fw_out,
          *, layer, seq, tt, rc):
    n, d = x.shape
    d_ff = fw_out.shape[1]
    vec = _layer_spec((1, d), layer)
    mat = _layer_spec((d, d), layer)
    nt = n // tt
    per = tt // HALO
    conv_tile = lambda s: jnp.minimum(s, nt - 1)
    cur = pl.BlockSpec((tt, d), lambda s: (conv_tile(s), 0))
    halo = pl.BlockSpec((HALO, d), lambda s: (jnp.maximum(conv_tile(s) * per - 1, 0), 0))
    prev = pl.BlockSpec((tt, d), lambda s: (jnp.maximum(s - 1, 0), 0))
    return pl.pallas_call(
        functools.partial(_tail_kernel, tt=tt, rc=rc, d_ff=d_ff, tiles_per_seq=seq // tt),
        grid=(nt + 1,),
        in_specs=[cur, halo, _layer_spec((CONV_K, d), layer), vec, vec, vec,
                  prev, prev, prev, prev, mat, mat, mat,
                  vec, _layer_spec((d, d_ff), layer, 0), _layer_spec((d, d_ff), layer, 1),
                  _layer_spec((d_ff, d), layer)],
        out_specs=prev,
        out_shape=jax.ShapeDtypeStruct((n, d), F32),
        scratch_shapes=[pltpu.VMEM((HALO + tt, d), F32),
                        pltpu.VMEM((SUBLANES - 1, HALO + tt, d), F32), pltpu.VMEM((tt, d), F32),
                        pltpu.VMEM((tt, d), BF16), pltpu.VMEM((tt, d), F32),
                        pltpu.VMEM((tt, d), BF16), pltpu.VMEM((tt, d), BF16),
                        pltpu.VMEM((tt, FFN_CHUNK), BF16)],
        compiler_params=_params("arbitrary"),
        name="tail",
    )(u, u, dw, dw_b, ln_g, ln_b, x, attn_o, gc, ga, wc, wa, wo, fg, fw_in, fw_in, fw_out)


def _attn_kernel(qt_ref, k_ref, vt_ref, lq_ref, lk_ref, sub_ref, o_ref, *scratch,
                 seq, tq, tk, cw, lambda_init):
    i = pl.program_id(2)
    npiece = tq // cw
    nchain = 2 * npiece
    kp = tk // cw
    ntile = tq // tk
    qs_refs, m_refs, l_refs, acc_refs = (scratch[n * nchain:(n + 1) * nchain] for n in range(4))
    row = lax.broadcasted_iota(jnp.int32, (V_DIM, cw), 0)
    zero = jnp.zeros((V_DIM, cw), BF16)
    ones = jnp.ones((2 * SUBLANES, tk), BF16)
    for p in range(npiece):
        piece = qt_ref[p]
        qs_refs[p][...] = jnp.where(row < HEAD_DIM, piece, zero)
        qs_refs[npiece + p][...] = jnp.where(row < HEAD_DIM, zero, piece)
    for c in range(nchain):
        m_refs[c][...] = jnp.full((1, cw), -jnp.inf, F32)
        l_refs[c][...] = jnp.zeros((1, cw), F32)
        acc_refs[c][...] = jnp.zeros((V_DIM, cw), F32)

    def scores(step):
        j, c, _ = step
        kt = k_ref[pl.ds(pl.multiple_of(j * tk, tk), tk), :]
        return _dot(kt, qs_refs[c][...])

    def update(step, s):
        j, c, diag = step
        if diag is not None:
            kpos = diag * tk + lax.broadcasted_iota(jnp.int32, s.shape, 0)
            qpos = (c % npiece) * cw + lax.broadcasted_iota(jnp.int32, s.shape, 1)
            s = jnp.where(kpos <= qpos, s, -jnp.inf)
        m_old = m_refs[c][...]
        m_new = jnp.maximum(m_old, jnp.max(s, axis=0, keepdims=True))
        alpha = jnp.exp2(m_old - m_new)
        p = jnp.exp2(s - m_new).astype(BF16)
        vt = jnp.concatenate([vt_ref[j * kp + t] for t in range(kp)], axis=1)
        pv = _dot(jnp.concatenate([vt, ones], axis=0), p)
        l_refs[c][...] = alpha * l_refs[c][...] + pv[V_DIM:V_DIM + 1]
        acc_refs[c][...] = alpha * acc_refs[c][...] + pv[0:V_DIM]
        m_refs[c][...] = m_new

    def run(steps):
        pending = [scores(st) for st in steps[:LOOKAHEAD]]
        for n, st in enumerate(steps):
            s = pending.pop(0)
            update(st, s)
            if n + LOOKAHEAD < len(steps):
                pending.append(scores(steps[n + LOOKAHEAD]))

    def body(it, carry):
        run([(it * ntile + t, c, None) for t in range(ntile) for c in range(nchain)])
        return carry

    if seq > tq:
        lax.fori_loop(0, i, body, 0)
    todo = {c: [] for c in range(nchain)}
    for t in range(ntile):
        for c in range(nchain):
            q_lo = (c % npiece) * cw
            if t * tk <= q_lo + cw - 1:
                todo[c].append((i * ntile + t, c, t if (t + 1) * tk - 1 > q_lo else None))
    diag_steps, last_at = [], {c: -CHAIN_GAP for c in todo}
    while any(todo.values()):
        live = [c for c in todo if todo[c]]
        rested = [c for c in live if len(diag_steps) - last_at[c] >= CHAIN_GAP]
        c = max(rested or live, key=lambda c: (len(todo[c]), -last_at[c]))
        last_at[c] = len(diag_steps)
        diag_steps.append(todo[c].pop(0))
    run(diag_steps)

    lq = lq_ref[...]
    lk = lk_ref[...]
    lam = (jnp.exp(jnp.sum(lq[0:1] * lk[0:1], axis=-1, keepdims=True))
           - jnp.exp(jnp.sum(lq[1:2] * lk[1:2], axis=-1, keepdims=True)) + lambda_init)
    gain = sub_ref[...] * (1.0 - lambda_init)
    for p in range(npiece):
        w0 = 1.0 / l_refs[p][...]
        w1 = lam / l_refs[npiece + p][...]
        o = acc_refs[p][...] * w0 - acc_refs[npiece + p][...] * w1
        ms = jnp.mean(o * o, axis=0, keepdims=True)
        o = o * lax.rsqrt(ms + EPS) * gain
        o_ref[p * cw:(p + 1) * cw, :] = o.T.astype(BF16)


def _attn(qt, k, vt, lam_q, lam_k, subln, *, layer, batch, seq, tq, tk, lambda_init):
    d = k.shape[-1]
    cw = qt.shape[-1]
    npc = seq // cw
    nchain = 2 * tq // cw
    qt4 = qt.reshape(batch, npc, d, cw)
    vt4 = vt.reshape(batch, npc, d, cw)
    k3 = k.reshape(batch, seq, d)
    qspec = pl.BlockSpec((None, tq // cw, V_DIM, cw), lambda b, h, i: (b, i, h, 0))
    kspec = pl.BlockSpec((None, seq, V_DIM), lambda b, h, i: (b, 0, h))
    vspec = pl.BlockSpec((None, npc, V_DIM, cw), lambda b, h, i: (b, 0, h, 0))
    ospec = pl.BlockSpec((None, tq, V_DIM), lambda b, h, i: (b, i, h))
    small = lambda shape: pl.BlockSpec((None,) + shape, lambda b, h, i: (layer, 0, 0))
    out = pl.pallas_call(
        functools.partial(_attn_kernel, seq=seq, tq=tq, tk=tk, cw=cw, lambda_init=lambda_init),
        grid=(batch, N_HEADS, seq // tq),
        in_specs=[qspec, kspec, vspec, small((2, HEAD_DIM)), small((2, HEAD_DIM)),
                  small((V_DIM, 1))],
        out_specs=ospec,
        out_shape=jax.ShapeDtypeStruct((batch, seq, d), BF16),
        scratch_shapes=([pltpu.VMEM((V_DIM, cw), BF16)] * nchain + [pltpu.VMEM((1, cw), F32)] * nchain
                        + [pltpu.VMEM((1, cw), F32)] * nchain
                        + [pltpu.VMEM((V_DIM, cw), F32)] * nchain),
        compiler_params=_params("parallel", "parallel", "arbitrary"),
        name="attn",
    )(qt4, k3, vt4, lam_q, lam_k, subln)
    return out.reshape(batch * seq, d)


def kernel(x, ffn1_norm, ffn1_w_in, ffn1_w_out, mix_norm, w_in, conv_dw, conv_dw_b, conv_ln_g,
           conv_ln_b, conv_w_out, q_norm, k_norm, lam_q, lam_k, attn_subln, attn_w_out, w_out,
           ffn2_norm, ffn2_w_in, ffn2_w_out):
    batch, seq, d = x.shape
    depth = w_in.shape[0]
    assert d == D_MODEL and seq % 512 == 0
    xs = x.reshape(batch * seq, d)
    rows = lambda p: p.reshape(depth, 1, -1)
    b16 = lambda p: p.astype(BF16)
    ffn1_w_in, ffn1_w_out, w_in = b16(ffn1_w_in), b16(ffn1_w_out), b16(w_in)
    conv_w_out, attn_w_out, w_out = b16(conv_w_out), b16(attn_w_out), b16(w_out)
    ffn2_w_in, ffn2_w_out = b16(ffn2_w_in), b16(ffn2_w_out)
    qg = rows(jnp.tile(q_norm, (1, 2 * N_HEADS)))
    kg = rows(jnp.tile(k_norm, (1, 2 * N_HEADS)))
    subln = attn_subln.reshape(depth, V_DIM, 1)
    for l in range(depth):
        lambda_init = 0.8 - 0.6 * math.exp(-0.3 * l)
        xs = _ffn(xs, rows(ffn1_norm), ffn1_w_in, ffn1_w_out, layer=l, tm=FFN_TM)
        u, qt, k, vt, gc, ga = _proj(xs, rows(mix_norm), w_in, qg, kg, layer=l, tm=ATTN_CW)
        attn_o = _attn(qt, k, vt, lam_q, lam_k, subln, layer=l, batch=batch, seq=seq,
                       tq=ATTN_TQ, tk=ATTN_TK, lambda_init=lambda_init)
        xs = _tail(u, xs, attn_o, gc, ga, conv_dw, rows(conv_dw_b), rows(conv_ln_g),
                   rows(conv_ln_b), conv_w_out, attn_w_out, w_out, rows(ffn2_norm), ffn2_w_in,
                   ffn2_w_out, layer=l, seq=seq, tt=TAIL_TT, rc=CONV_RC)
    return xs.reshape(batch, seq, d)
```

```python
import functools
import math

import jax
import jax.numpy as jnp
from jax import lax
from jax.experimental import pallas as pl
from jax.experimental.pallas import tpu as pltpu

D_MODEL = 1024
N_HEADS = 8
HEAD_DIM = 64
V_DIM = 2 * HEAD_DIM
CONV_K = 31
EPS = 1e-6

VMEM_LIMIT_BYTES = 56 * 1024 * 1024
SUBLANES, LANES = 8, 128
FFN_CHUNK = 256
HALO = 32
FFN_TM = 512
TAIL_TT = 256
CONV_RC = 64
ATTN_TQ = 4096
ATTN_TK = 256
ATTN_CW = 256
LOOKAHEAD = 6
CHAIN_GAP = 8
Q_SCALE =HEAD_DIM ** -0.5 * math.log2(math.e)

F32 = jnp.float32
BF16 = jnp.bfloat16


def _params(*sem):
    return pltpu.CompilerParams(dimension_semantics=sem, vmem_limit_bytes=VMEM_LIMIT_BYTES)


def _layer_spec(shape, layer, col=0):
    idx = (layer,) + (0,) * (len(shape) - 1) + (col,)
    return pl.BlockSpec((None,) + tuple(shape), lambda *_: idx, pipeline_mode=pl.Buffered(1))


def _rms_rows(x, g):
    ms = jnp.mean(x * x, axis=-1, keepdims=True)
    return x * lax.rsqrt(ms + EPS) * g


def _dot(a, b):
    return jnp.dot(a, b, preferred_element_type=F32)


def _ffn_chunk(h, c, wa_ref, wb_ref, wo_ref, acc_ref):
    sl = slice(c * FFN_CHUNK, (c + 1) * FFN_CHUNK)
    a = _dot(h, wa_ref[:, sl])
    b = _dot(h, wb_ref[:, sl])
    gated = (a * jax.nn.sigmoid(a) * b).astype(BF16)
    contrib = _dot(gated, wo_ref[sl, :])
    if c == 0:
        acc_ref[...] = contrib
    else:
        acc_ref[...] += contrib
    return contrib[0:SUBLANES, 0:LANES]


def _ffn_kernel(x_ref, g_ref, wa_ref, wb_ref, wo_ref, o_ref, acc_ref, *, d_ff):
    x = x_ref[...]
    h = _rms_rows(x, g_ref[...]).astype(BF16)
    for c in range(d_ff // FFN_CHUNK):
        _ffn_chunk(h, c, wa_ref, wb_ref, wo_ref, acc_ref)
    o_ref[...] = x + 0.5 * acc_ref[...]


def _ffn(x, g, w_in, w_out, *, layer, tm):
    n, d = x.shape
    d_ff = w_out.shape[1]
    row = pl.BlockSpec((tm, d), lambda i: (i, 0))
    return pl.pallas_call(
        functools.partial(_ffn_kernel, d_ff=d_ff),
        grid=(n // tm,),
        in_specs=[
            row,
            _layer_spec((1, d), layer),
            _layer_spec((d, d_ff), layer, 0),
            _layer_spec((d, d_ff), layer, 1),
            _layer_spec((d_ff, d), layer),
        ],
        out_specs=row,
        out_shape=jax.ShapeDtypeStruct((n, d), F32),
        scratch_shapes=[pltpu.VMEM((tm, d), F32)],
        compiler_params=_params("parallel"),
        name="ffn",
    )(x, g, w_in, w_in, w_out)


def _qk_norm(p, gain):
    outs = []
    for hh in range(N_HEADS):
        slab = p[:, hh * V_DIM:(hh + 1) * V_DIM]
        sq = slab * slab
        lane = lax.broadcasted_iota(jnp.int32, slab.shape, 1)
        first = lane < HEAD_DIM
        s0 = jnp.sum(jnp.where(first, sq, 0.0), axis=-1, keepdims=True)
        s1 = jnp.sum(jnp.where(first, 0.0, sq), axis=-1, keepdims=True)
        inv = jnp.where(first, lax.rsqrt(s0 / HEAD_DIM + EPS), lax.rsqrt(s1 / HEAD_DIM + EPS))
        outs.append(slab * inv * gain[:, hh * V_DIM:(hh + 1) * V_DIM])
    return outs


def _proj_kernel(x_ref, g_ref, w_ref, qg_ref, kg_ref,
                 u_ref, qt_ref, k_ref, vt_ref, gc_ref, ga_ref):
    d = D_MODEL
    h = _rms_rows(x_ref[...], g_ref[...]).astype(BF16)
    a = _dot(h, w_ref[:, 0:d])
    gte = _dot(h, w_ref[:, d:2 * d])
    u_ref[...] = a * jax.nn.sigmoid(gte)
    q = _dot(h, w_ref[:, 2 * d:3 * d])
    for hh, qh in enumerate(_qk_norm(q, qg_ref[...])):
        qt_ref[hh * V_DIM:(hh + 1) * V_DIM, :] = (qh * Q_SCALE).T.astype(BF16)
    k = _dot(h, w_ref[:, 3 * d:4 * d])
    for hh, kh in enumerate(_qk_norm(k, kg_ref[...])):
        k_ref[:, hh * V_DIM:(hh + 1) * V_DIM] = kh.astype(BF16)
    vt_ref[...] = _dot(h, w_ref[:, 4 * d:5 * d]).T.astype(BF16)
    gc_ref[...] = jax.nn.sigmoid(_dot(h, w_ref[:, 5 * d:6 * d]))
    ga_ref[...] = jax.nn.sigmoid(_dot(h, w_ref[:, 6 * d:7 * d]))


def _proj(x, g, w, qg, kg, *, layer, tm):
    n, d = x.shape
    row = pl.BlockSpec((tm, d), lambda i: (i, 0))
    tr = pl.BlockSpec((None, d, tm), lambda i: (i, 0, 0))
    f32o = jax.ShapeDtypeStruct((n, d), F32)
    b16o = jax.ShapeDtypeStruct((n, d), BF16)
    tro = jax.ShapeDtypeStruct((n // tm, d, tm), BF16)
    return pl.pallas_call(
        _proj_kernel,
        grid=(n // tm,),
        in_specs=[row, _layer_spec((1, d), layer), _layer_spec(w.shape[1:], layer),
                  _layer_spec((1, d), layer), _layer_spec((1, d), layer)],
        out_specs=[row, tr, row, tr, row, row],
        out_shape=[f32o, tro, b16o, tro, f32o, f32o],
        compiler_params=_params("parallel"),
        name="proj",
    )(x, g, w, qg, kg)


def _conv_units(win_ref, wsh_ref, dw_ref, b_ref, lg_ref, lb_ref, y_ref, act_ref, *, tt, rc):
    first = HALO - (CONV_K - 1)
    rows = HALO + tt

    def shift(lanes, phase, after=None):
        del after
        moved = pltpu.roll(win_ref[:, lanes], rows - phase, axis=0)
        wsh_ref[phase - 1, :, lanes] = moved
        return moved[0:SUBLANES]

    def depthwise(r0, lanes, after=None):
        bias = jnp.broadcast_to(b_ref[:, lanes], (SUBLANES, LANES))
        if after is not None:
            bias = _after(bias, [after])
        acc = jnp.concatenate([bias] * (rc // SUBLANES), axis=0)
        for tap in range(CONV_K):
            a, phase = divmod(first + tap, SUBLANES)
            lo = r0 + SUBLANES * a
            src = win_ref[lo:lo + rc, lanes] if phase == 0 else wsh_ref[phase - 1, lo:lo + rc, lanes]
            acc = acc + dw_ref[tap:tap + 1, lanes] * src
        y_ref[r0:r0 + rc, lanes] = acc
        return acc[0:SUBLANES]

    def norm_act(rows, after=None):
        del after
        y = y_ref[rows, :]
        mu = jnp.mean(y, axis=-1, keepdims=True)
        yc = y - mu
        var = jnp.mean(yc * yc, axis=-1, keepdims=True)
        y = yc * lax.rsqrt(var + EPS) * lg_ref[...] + lb_ref[...]
        y = y * jax.nn.sigmoid(y)
        act_ref[rows, :] = y.astype(BF16)
        return y[0:SUBLANES, 0:LANES]

    units = []
    for lt in range(D_MODEL // LANES):
        lanes = slice(lt * LANES, (lt + 1) * LANES)
        units += [functools.partial(shift, lanes, p) for p in range(1, SUBLANES)]
        units += [functools.partial(depthwise, c * rc, lanes) for c in range(tt // rc)]
    units += [functools.partial(norm_act, slice(c * rc, (c + 1) * rc)) for c in range(tt // rc)]
    return units


def _zero_bits(values):
    zero = None
    for v in values:
        bits = pltpu.bitcast(v, jnp.uint32)
        z = lax.shift_right_logical(lax.shift_right_logical(bits, jnp.uint32(16)), jnp.uint32(16))
        zero = z if zero is None else zero | z
    return zero


def _after(x, values):
    return pltpu.bitcast(pltpu.bitcast(x, jnp.uint32) + _zero_bits(values), x.dtype)


def _order_after(ref, values):
    if values:
        rows = SUBLANES * 4 // ref.dtype.itemsize
        ref[0:rows, 0:LANES] = _after(ref[0:rows, 0:LANES], values)


def _tail_kernel(u_ref, halo_ref, dw_ref, b_ref, lg_ref, lb_ref,
                 x_ref, a_ref, gc_ref, ga_ref, wc_ref, wa_ref, wo_ref,
                 fg_ref, fwa_ref, fwb_ref, fwo_ref,
                 o_ref, win_ref, wsh_ref, y_ref, act_ref, acc_ref, mg_ref, h_ref, gt_ref,
                 *, tt, rc, d_ff, tiles_per_seq):
    s = pl.program_id(0)
    d = D_MODEL

    @pl.when(s == 0)
    def _():
        act_ref[...] = jnp.zeros(act_ref.shape, BF16)

    prev_act = act_ref[...]
    win_ref[0:HALO, :] = jnp.where(s % tiles_per_seq != 0, halo_ref[...], 0.0)
    win_ref[HALO:HALO + tt, :] = u_ref[...]
    conv = _conv_units(win_ref, wsh_ref, dw_ref, b_ref, lg_ref, lb_ref, y_ref, act_ref, tt=tt, rc=rc)
    nchunk = d_ff // FFN_CHUNK
    cover = [2 * d * d, d * d] + [2 * d * FFN_CHUNK, FFN_CHUNK * d] * nchunk
    edges = [round(len(conv) * sum(cover[:g]) / sum(cover)) for g in range(len(cover) + 1)]
    groups = iter([conv[lo:hi] for lo, hi in zip(edges[:-1], edges[1:])])
    tile = lambda v: v[0:SUBLANES, 0:LANES]
    run_group = lambda after: [unit(after=after) for unit in next(groups)]

    yc = _dot(prev_act, wc_ref[...])
    ya = _dot(a_ref[...], wa_ref[...])
    mg_ref[...] = (gc_ref[...] * yc + ga_ref[...] * ya).astype(BF16)
    _order_after(mg_ref, run_group(None))
    proj = _dot(mg_ref[...], wo_ref[...])
    x = x_ref[...] + proj
    h_ref[...] = _rms_rows(x, fg_ref[...]).astype(BF16)
    _order_after(h_ref, run_group(tile(ya)))
    last = tile(proj)
    for c in range(nchunk):
        sl = slice(c * FFN_CHUNK, (c + 1) * FFN_CHUNK)
        h = h_ref[...]
        a = _dot(h, fwa_ref[:, sl])
        b = _dot(h, fwb_ref[:, sl])
        gt_ref[...] = (a * jax.nn.sigmoid(a) * b).astype(BF16)
        _order_after(gt_ref, run_group(last))
        contrib = _dot(gt_ref[...], fwo_ref[sl, :])
        if c == 0:
            acc_ref[...] = contrib
        else:
            acc_ref[...] += contrib
        _order_after(acc_ref, run_group(tile(b)))
        last = tile(contrib)
    o_ref[...] = x + 0.5 * acc_ref[...]


def _tail(u, x, attn_o, gc, ga, dw, dw_b, ln_g, ln_b, wc, wa, wo, fg, fw_in, fw_out,
          *, layer, seq, tt, rc):
    n, d = x.shape
    d_ff = fw_out.shape[1]
    vec = _layer_spec((1, d), layer)
    mat = _layer_spec((d, d), layer)
    nt = n // tt
    per = tt // HALO
    conv_tile = lambda s: jnp.minimum(s, nt - 1)
    cur = pl.BlockSpec((tt, d), lambda s: (conv_tile(s), 0))
    halo = pl.BlockSpec((HALO, d), lambda s: (jnp.maximum(conv_tile(s) * per - 1, 0), 0))
    prev = pl.BlockSpec((tt, d), lambda s: (jnp.maximum(s - 1, 0), 0))
    return pl.pallas_call(
        functools.partial(_tail_kernel, tt=tt, rc=rc, d_ff=d_ff, tiles_per_seq=seq // tt),
        grid=(nt + 1,),
        in_specs=[cur, halo, _layer_spec((CONV_K, d), layer), vec, vec, vec,
                  prev, prev, prev, prev, mat, mat, mat,
                  vec, _layer_spec((d, d_ff), layer, 0), _layer_spec((d, d_ff), layer, 1),
                  _layer_spec((d_ff, d), layer)],
        out_specs=prev,
        out_shape=jax.ShapeDtypeStruct((n, d), F32),
        scratch_shapes=[pltpu.VMEM((HALO + tt, d), F32),
                        pltpu.VMEM((SUBLANES - 1, HALO + tt, d), F32), pltpu.VMEM((tt, d), F32),
                        pltpu.VMEM((tt, d), BF16), pltpu.VMEM((tt, d), F32),
                        pltpu.VMEM((tt, d), BF16), pltpu.VMEM((tt, d), BF16),
                        pltpu.VMEM((tt, FFN_CHUNK), BF16)],
        compiler_params=_params("arbitrary"),
        name="tail",
    )(u, u, dw, dw_b, ln_g, ln_b, x, attn_o, gc, ga, wc, wa, wo, fg, fw_in, fw_in, fw_out)


def _attn_kernel(qt_ref, k_ref, vt_ref, lq_ref, lk_ref, sub_ref, o_ref, *scratch,
                 seq, tq, tk, cw, lambda_init):
    i = pl.program_id(2)
    npiece = tq // cw
    nchain = 2 * npiece
    kp = tk // cw
    ntile = tq // tk
    qs_refs, m_refs, l_refs, acc_refs = (scratch[n * nchain:(n + 1) * nchain] for n in range(4))
    row = lax.broadcasted_iota(jnp.int32, (V_DIM, cw), 0)
    zero = jnp.zeros((V_DIM, cw), BF16)
    ones = jnp.ones((2 * SUBLANES, tk), BF16)
    for p in range(npiece):
        piece = qt_ref[p]
        qs_refs[p][...] = jnp.where(row < HEAD_DIM, piece, zero)
        qs_refs[npiece + p][...] = jnp.where(row < HEAD_DIM, zero, piece)
    for c in range(nchain):
        m_refs[c][...] = jnp.full((1, cw), -jnp.inf, F32)
        l_refs[c][...] = jnp.zeros((1, cw), F32)
        acc_refs[c][...] = jnp.zeros((V_DIM, cw), F32)

    def scores(step):
        j, c, _ = step
        kt = k_ref[pl.ds(pl.multiple_of(j * tk, tk), tk), :]
        return _dot(kt, qs_refs[c][...])

    def update(step, s):
        j, c, diag = step
        if diag is not None:
            kpos = diag * tk + lax.broadcasted_iota(jnp.int32, s.shape, 0)
            qpos = (c % npiece) * cw + lax.broadcasted_iota(jnp.int32, s.shape, 1)
            s = jnp.where(kpos <= qpos, s, -jnp.inf)
        m_old = m_refs[c][...]
        m_new = jnp.maximum(m_old, jnp.max(s, axis=0, keepdims=True))
        alpha = jnp.exp2(m_old - m_new)
        p = jnp.exp2(s - m_new).astype(BF16)
        vt = jnp.concatenate([vt_ref[j * kp + t] for t in range(kp)], axis=1)
        pv = _dot(jnp.concatenate([vt, ones], axis=0), p)
        l_refs[c][...] = alpha * l_refs[c][...] + pv[V_DIM:V_DIM + 1]
        acc_refs[c][...] = alpha * acc_refs[c][...] + pv[0:V_DIM]
        m_refs[c][...] = m_new

    def run(steps):
        pending = [scores(st) for st in steps[:LOOKAHEAD]]
        for n, st in enumerate(steps):
            s = pending.pop(0)
            update(st, s)
            if n + LOOKAHEAD < len(steps):
                pending.append(scores(steps[n + LOOKAHEAD]))

    def body(it, carry):
        run([(it * ntile + t, c, None) for t in range(ntile) for c in range(nchain)])
        return carry

    if seq > tq:
        lax.fori_loop(0, i, body, 0)
    todo = {c: [] for c in range(nchain)}
    for t in range(ntile):
        for c in range(nchain):
            q_lo = (c % npiece) * cw
            if t * tk <= q_lo + cw - 1:
                todo[c].append((i * ntile + t, c, t if (t + 1) * tk - 1 > q_lo else None))
    diag_steps, last_at = [], {c: -CHAIN_GAP for c in todo}
    while any(todo.values()):
        live = [c for c in todo if todo[c]]
        rested = [c for c in live if len(diag_steps) - last_at[c] >= CHAIN_GAP]
        c = max(rested or live, key=lambda c: (len(todo[c]), -last_at[c]))
        last_at[c] = len(diag_steps)
        diag_steps.append(todo[c].pop(0))
    run(diag_steps)

    lq = lq_ref[...]
    lk = lk_ref[...]
    lam = (jnp.exp(jnp.sum(lq[0:1] * lk[0:1], axis=-1, keepdims=True))
           - jnp.exp(jnp.sum(lq[1:2] * lk[1:2], axis=-1, keepdims=True)) + lambda_init)
    gain = sub_ref[...] * (1.0 - lambda_init)
    for p in range(npiece):
        w0 = 1.0 / l_refs[p][...]
        w1 = lam / l_refs[npiece + p][...]
        o = acc_refs[p][...] * w0 - acc_refs[npiece + p][...] * w1
        ms = jnp.mean(o * o, axis=0, keepdims=True)
        o = o * lax.rsqrt(ms + EPS) * gain
        o_ref[p * cw:(p + 1) * cw, :] = o.T.astype(BF16)


def _attn(qt, k, vt, lam_q, lam_k, subln, *, layer, batch, seq, tq, tk, lambda_init):
    d = k.shape[-1]
    cw = qt.shape[-1]
    npc = seq // cw
    nchain = 2 * tq // cw
    qt4 = qt.reshape(batch, npc, d, cw)
    vt4 = vt.reshape(batch, npc, d, cw)
    k3 = k.reshape(batch, seq, d)
    qspec = pl.BlockSpec((None, tq // cw, V_DIM, cw), lambda b, h, i: (b, i, h, 0))
    kspec = pl.BlockSpec((None, seq, V_DIM), lambda b, h, i: (b, 0, h))
    vspec = pl.BlockSpec((None, npc, V_DIM, cw), lambda b, h, i: (b, 0, h, 0))
    ospec = pl.BlockSpec((None, tq, V_DIM), lambda b, h, i: (b, i, h))
    small = lambda shape: pl.BlockSpec((None,) + shape, lambda b, h, i: (layer, 0, 0))
    out = pl.pallas_call(
        functools.partial(_attn_kernel, seq=seq, tq=tq, tk=tk, cw=cw, lambda_init=lambda_init),
        grid=(batch, N_HEADS, seq // tq),
        in_specs=[qspec, kspec, vspec, small((2, HEAD_DIM)), small((2, HEAD_DIM)),
                  small((V_DIM, 1))],
        out_specs=ospec,
        out_shape=jax.ShapeDtypeStruct((batch, seq, d), BF16),
        scratch_shapes=([pltpu.VMEM((V_DIM, cw), BF16)] * nchain + [pltpu.VMEM((1, cw), F32)] * nchain
                        + [pltpu.VMEM((1, cw), F32)] * nchain
                        + [pltpu.VMEM((V_DIM, cw), F32)] * nchain),
        compiler_params=_params("parallel", "parallel", "arbitrary"),
        name="attn",
    )(qt4, k3, vt4, lam_q, lam_k, subln)
    return out.reshape(batch * seq, d)


def kernel(x, ffn1_norm, ffn1_w_in, ffn1_w_out, mix_norm, w_in, conv_dw, conv_dw_b, conv_ln_g,
           conv_ln_b, conv_w_out, q_norm, k_norm, lam_q, lam_k, attn_subln, attn_w_out, w_out,
           ffn2_norm, ffn2_w_in, ffn2_w_out):
    batch, seq, d = x.shape
    depth = w_in.shape[0]
    assert d == D_MODEL and seq % 512 == 0
    xs = x.reshape(batch * seq, d)
    rows = lambda p: p.reshape(depth, 1, -1)
    b16 = lambda p: p.astype(BF16)
    ffn1_w_in, ffn1_w_out, w_in = b16(ffn1_w_in), b16(ffn1_w_out), b16(w_in)
    conv_w_out, attn_w_out, w_out = b16(conv_w_out), b16(attn_w_out), b16(w_out)
    ffn2_w_in, ffn2_w_out = b16(ffn2_w_in), b16(ffn2_w_out)
    qg = rows(jnp.tile(q_norm, (1, 2 * N_HEADS)))
    kg = rows(jnp.tile(k_norm, (1, 2 * N_HEADS)))
    subln = attn_subln.reshape(depth, V_DIM, 1)
    for l in range(depth):
        lambda_init = 0.8 - 0.6 * math.exp(-0.3 * l)
        xs = _ffn(xs, rows(ffn1_norm), ffn1_w_in, ffn1_w_out, layer=l, tm=FFN_TM)
        u, qt, k, vt, gc, ga = _proj(xs, rows(mix_norm), w_in, qg, kg, layer=l, tm=ATTN_CW)
        attn_o = _attn(qt, k, vt, lam_q, lam_k, subln, layer=l, batch=batch, seq=seq,
                       tq=ATTN_TQ, tk=ATTN_TK, lambda_init=lambda_init)
        xs = _tail(u, xs, attn_o, gc, ga, conv_dw, rows(conv_dw_b), rows(conv_ln_g),
                   rows(conv_ln_b), conv_w_out, attn_w_out, w_out, rows(ffn2_norm), ffn2_w_in,
                   ffn2_w_out, layer=l, seq=seq, tt=TAIL_TT, rc=CONV_RC)
    return xs.reshape(batch, seq, d)
```

```python
import functools
import math

import jax
import jax.numpy as jnp
from jax import lax
from jax.experimental import pallas as pl
from jax.experimental.pallas import tpu as pltpu

D_MODEL = 1024
N_HEADS = 8
HEAD_DIM = 64
V_DIM = 2 * HEAD_DIM
CONV_K = 31
EPS = 1e-6

VMEM_LIMIT_BYTES = 56 * 1024 * 1024
SUBLANES, LANES = 8, 128
FFN_CHUNK = 256
HALO = 32
FFN_TM = 512
TAIL_TT = 256
CONV_RC = 64
ATTN_TQ = 2048
ATTN_TK = 256
ATTN_CW = 256
LOOKAHEAD = 6
CHAIN_GAP = 8
Q_SCALE = HEAD_DIM ** -0.5 * math.log2(math.e)

F32 = jnp.float32
BF16 = jnp.bfloat16


def _params(*sem):
    return pltpu.CompilerParams(dimension_semantics=sem, vmem_limit_bytes=VMEM_LIMIT_BYTES)


def _layer_spec(shape, layer, col=0):
    idx = (layer,) + (0,) * (len(shape) - 1) + (col,)
    return pl.BlockSpec((None,) + tuple(shape), lambda *_: idx, pipeline_mode=pl.Buffered(1))


def _rms_rows(x, g):
    ms = jnp.mean(x * x, axis=-1, keepdims=True)
    return x * lax.rsqrt(ms + EPS) * g


def _dot(a, b):
    return jnp.dot(a, b, preferred_element_type=F32)


def _ffn_chunk(h, c, wa_ref, wb_ref, wo_ref, acc_ref):
    sl = slice(c * FFN_CHUNK, (c + 1) * FFN_CHUNK)
    a = _dot(h, wa_ref[:, sl])
    b = _dot(h, wb_ref[:, sl])
    gated = (a * jax.nn.sigmoid(a) * b).astype(BF16)
    contrib = _dot(gated, wo_ref[sl, :])
    if c == 0:
        acc_ref[...] = contrib
    else:
        acc_ref[...] += contrib


def _ffn_kernel(x_ref, g_ref, wa_ref, wb_ref, wo_ref, o_ref, acc_ref, *, d_ff):
    x = x_ref[...]
    h = _rms_rows(x, g_ref[...]).astype(BF16)
    for c in range(d_ff // FFN_CHUNK):
        _ffn_chunk(h, c, wa_ref, wb_ref, wo_ref, acc_ref)
    o_ref[...] = x + 0.5 * acc_ref[...]


def _ffn(x, g, w_in, w_out, *, layer, tm):
    n, d = x.shape
    d_ff = w_out.shape[1]
    row = pl.BlockSpec((tm, d), lambda i: (i, 0))
    return pl.pallas_call(
        functools.partial(_ffn_kernel, d_ff=d_ff),
        grid=(n // tm,),
        in_specs=[
            row,
            _layer_spec((1, d), layer),
            _layer_spec((d, d_ff), layer, 0),
            _layer_spec((d, d_ff), layer, 1),
            _layer_spec((d_ff, d), layer),
        ],
        out_specs=row,
        out_shape=jax.ShapeDtypeStruct((n, d), F32),
        scratch_shapes=[pltpu.VMEM((tm, d), F32)],
        compiler_params=_params("parallel"),
        name="ffn",
    )(x, g, w_in, w_in, w_out)


def _qk_norm(p, gain):
    outs = []
    for hh in range(N_HEADS):
        slab = p[:, hh * V_DIM:(hh + 1) * V_DIM]
        sq = slab * slab
        lane = lax.broadcasted_iota(jnp.int32, slab.shape, 1)
        first = lane < HEAD_DIM
        s0 = jnp.sum(jnp.where(first, sq, 0.0), axis=-1, keepdims=True)
        s1 = jnp.sum(jnp.where(first, 0.0, sq), axis=-1, keepdims=True)
        inv = jnp.where(first, lax.rsqrt(s0 / HEAD_DIM + EPS), lax.rsqrt(s1 / HEAD_DIM + EPS))
        outs.append(slab * inv * gain[:, hh * V_DIM:(hh + 1) * V_DIM])
    return outs


def _proj_kernel(x_ref, g_ref, w_ref, qg_ref, kg_ref,
                 u_ref, qt_ref, k_ref, vt_ref, gc_ref, ga_ref):
    d = D_MODEL
    h = _rms_rows(x_ref[...], g_ref[...]).astype(BF16)
    a = _dot(h, w_ref[:, 0:d])
    gte = _dot(h, w_ref[:, d:2 * d])
    u_ref[...] = a * jax.nn.sigmoid(gte)
    q = _dot(h, w_ref[:, 2 * d:3 * d])
    for hh, qh in enumerate(_qk_norm(q, qg_ref[...])):
        qt_ref[hh * V_DIM:(hh + 1) * V_DIM, :] = (qh * Q_SCALE).T.astype(BF16)
    k = _dot(h, w_ref[:, 3 * d:4 * d])
    for hh, kh in enumerate(_qk_norm(k, kg_ref[...])):
        k_ref[:, hh * V_DIM:(hh + 1) * V_DIM] = kh.astype(BF16)
    vt_ref[...] = _dot(h, w_ref[:, 4 * d:5 * d]).T.astype(BF16)
    gc_ref[...] = jax.nn.sigmoid(_dot(h, w_ref[:, 5 * d:6 * d]))
    ga_ref[...] = jax.nn.sigmoid(_dot(h, w_ref[:, 6 * d:7 * d]))


def _proj(x, g, w, qg, kg, *, layer, tm):
    n, d = x.shape
    row = pl.BlockSpec((tm, d), lambda i: (i, 0))
    tr = pl.BlockSpec((None, d, tm), lambda i: (i, 0, 0))
    f32o = jax.ShapeDtypeStruct((n, d), F32)
    b16o = jax.ShapeDtypeStruct((n, d), BF16)
    tro = jax.ShapeDtypeStruct((n // tm, d, tm), BF16)
    return pl.pallas_call(
        _proj_kernel,
        grid=(n // tm,),
        in_specs=[row, _layer_spec((1, d), layer), _layer_spec(w.shape[1:], layer),
                  _layer_spec((1, d), layer), _layer_spec((1, d), layer)],
        out_specs=[row, tr, row, tr, row, row],
        out_shape=[f32o, tro, b16o, tro, f32o, f32o],
        compiler_params=_params("parallel"),
        name="proj",
    )(x, g, w, qg, kg)


def _conv_units(win_ref, wsh_ref, dw_ref, b_ref, lg_ref, lb_ref, y_ref, act_ref, *, tt, rc):
    first = HALO - (CONV_K - 1)
    rows = HALO + tt

    def shift(lanes, phase, after=None):
        del after
        moved = pltpu.roll(win_ref[:, lanes], rows - phase, axis=0)
        wsh_ref[phase - 1, :, lanes] = moved
        return moved[0:SUBLANES]

    def depthwise(r0, lanes, after=None):
        bias = jnp.broadcast_to(b_ref[:, lanes], (SUBLANES, LANES))
        if after is not None:
            bias = _after(bias, [after])
        acc = jnp.concatenate([bias] * (rc // SUBLANES), axis=0)
        for tap in range(CONV_K):
            a, phase = divmod(first + tap, SUBLANES)
            lo = r0 + SUBLANES * a
            src = win_ref[lo:lo + rc, lanes] if phase == 0 else wsh_ref[phase - 1, lo:lo + rc, lanes]
            acc = acc + dw_ref[tap:tap + 1, lanes] * src
        y_ref[r0:r0 + rc, lanes] = acc
        return acc[0:SUBLANES]

    def norm_act(rows, after=None):
        del after
        y = y_ref[rows, :]
        mu = jnp.mean(y, axis=-1, keepdims=True)
        yc = y - mu
        var = jnp.mean(yc * yc, axis=-1, keepdims=True)
        y = yc * lax.rsqrt(var + EPS) * lg_ref[...] + lb_ref[...]
        y = y * jax.nn.sigmoid(y)
        act_ref[rows, :] = y.astype(BF16)
        return y[0:SUBLANES, 0:LANES]

    units = []
    for lt in range(D_MODEL // LANES):
        lanes = slice(lt * LANES, (lt + 1) * LANES)
        units += [functools.partial(shift, lanes, p) for p in range(1, SUBLANES)]
        units += [functools.partial(depthwise, c * rc, lanes) for c in range(tt // rc)]
    units += [functools.partial(norm_act, slice(c * rc, (c + 1) * rc)) for c in range(tt // rc)]
    return units


def _zero_bits(values):
    zero = None
    for v in values:
        bits = pltpu.bitcast(v, jnp.uint32)
        z = lax.shift_right_logical(lax.shift_right_logical(bits, jnp.uint32(16)), jnp.uint32(16))
        zero = z if zero is None else zero | z
    return zero


def _after(x, values):
    return pltpu.bitcast(pltpu.bitcast(x, jnp.uint32) + _zero_bits(values), x.dtype)


def _order_after(ref, values):
    if values:
        rows = SUBLANES * 4 // ref.dtype.itemsize
        ref[0:rows, 0:LANES] = _after(ref[0:rows, 0:LANES], values)


def _tail_kernel(u_ref, halo_ref, dw_ref, b_ref, lg_ref, lb_ref,
                 x_ref, a_ref, gc_ref, ga_ref, wc_ref, wa_ref, wo_ref,
                 fg_ref, fwa_ref, fwb_ref, fwo_ref,
                 o_ref, win_ref, wsh_ref, y_ref, act_ref, acc_ref, mg_ref, h_ref, gt_ref,
                 *, tt, rc, d_ff, tiles_per_seq):
    s = pl.program_id(0)
    d = D_MODEL

    @pl.when(s == 0)
    def _():
        act_ref[...] = jnp.zeros(act_ref.shape, BF16)

    prev_act = act_ref[...]
    win_ref[0:HALO, :] = jnp.where(s % tiles_per_seq != 0, halo_ref[...], 0.0)
    win_ref[HALO:HALO + tt, :] = u_ref[...]
    conv = _conv_units(win_ref, wsh_ref, dw_ref, b_ref, lg_ref, lb_ref, y_ref, act_ref, tt=tt, rc=rc)
    nchunk = d_ff // FFN_CHUNK
    cover = [2 * d * d, d * d] + [2 * d * FFN_CHUNK, FFN_CHUNK * d] * nchunk
    edges = [round(len(conv) * sum(cover[:g]) / sum(cover)) for g in range(len(cover) + 1)]
    groups = iter([conv[lo:hi] for lo, hi in zip(edges[:-1], edges[1:])])
    tile = lambda v: v[0:SUBLANES, 0:LANES]
    run_group = lambda after: [unit(after=after) for unit in next(groups)]

    yc = _dot(prev_act, wc_ref[...])
    ya = _dot(a_ref[...], wa_ref[...])
    mg_ref[...] = (gc_ref[...] * yc + ga_ref[...] * ya).astype(BF16)
    _order_after(mg_ref, run_group(None))
    proj = _dot(mg_ref[...], wo_ref[...])
    x = x_ref[...] + proj
    h_ref[...] = _rms_rows(x, fg_ref[...]).astype(BF16)
    _order_after(h_ref, run_group(tile(ya)))
    last = tile(proj)
    for c in range(nchunk):
        sl = slice(c * FFN_CHUNK, (c + 1) * FFN_CHUNK)
        h = h_ref[...]
        a = _dot(h, fwa_ref[:, sl])
        b = _dot(h, fwb_ref[:, sl])
        gt_ref[...] = (a * jax.nn.sigmoid(a) * b).astype(BF16)
        _order_after(gt_ref, run_group(last))
        contrib = _dot(gt_ref[...], fwo_ref[sl, :])
        if c == 0:
            acc_ref[...] = contrib
        else:
            acc_ref[...] += contrib
        _order_after(acc_ref, run_group(tile(b)))
        last = tile(contrib)
    o_ref[...] = x + 0.5 * acc_ref[...]


def _tail(u, x, attn_o, gc, ga, dw, dw_b, ln_g, ln_b, wc, wa, wo, fg, fw_in, fw_out,
          *, layer, seq, tt, rc):
    n, d = x.shape
    d_ff = fw_out.shape[1]
    vec = _layer_spec((1, d), layer)
    mat = _layer_spec((d, d), layer)
    nt = n // tt
    per = tt // HALO
    conv_tile = lambda s: jnp.minimum(s, nt - 1)
    cur = pl.BlockSpec((tt, d), lambda s: (conv_tile(s), 0))
    halo = pl.BlockSpec((HALO, d), lambda s: (jnp.maximum(conv_tile(s) * per - 1, 0), 0))
    prev = pl.BlockSpec((tt, d), lambda s: (jnp.maximum(s - 1, 0), 0))
    return pl.pallas_call(
        functools.partial(_tail_kernel, tt=tt, rc=rc, d_ff=d_ff, tiles_per_seq=seq // tt),
        grid=(nt + 1,),
        in_specs=[cur, halo, _layer_spec((CONV_K, d), layer), vec, vec, vec,
                  prev, prev, prev, prev, mat, mat, mat,
                  vec, _layer_spec((d, d_ff), layer, 0), _layer_spec((d, d_ff), layer, 1),
                  _layer_spec((d_ff, d), layer)],
        out_specs=prev,
        out_shape=jax.ShapeDtypeStruct((n, d), F32),
        scratch_shapes=[pltpu.VMEM((HALO + tt, d), F32),
                        pltpu.VMEM((SUBLANES - 1, HALO + tt, d), F32), pltpu.VMEM((tt, d), F32),
                        pltpu.VMEM((tt, d), BF16), pltpu.VMEM((tt, d), F32),
                        pltpu.VMEM((tt, d), BF16), pltpu.VMEM((tt, d), BF16),
                        pltpu.VMEM((tt, FFN_CHUNK), BF16)],
        compiler_params=_params("arbitrary"),
        name="tail",
    )(u, u, dw, dw_b, ln_g, ln_b, x, attn_o, gc, ga, wc, wa, wo, fg, fw_in, fw_in, fw_out)


def _attn_kernel(qt_ref, k_ref, vt_ref, lq_ref, lk_ref, sub_ref, o_ref, *scratch,
                 seq, tq, tk, cw, lambda_init):
    i = pl.program_id(2)
    npiece = tq // cw
    nchain = 2 * npiece
    kp = tk // cw
    ntile = tq // tk
    qs_refs, m_refs, l_refs, acc_refs = (scratch[n * nchain:(n + 1) * nchain] for n in range(4))
    row = lax.broadcasted_iota(jnp.int32, (V_DIM, cw), 0)
    zero = jnp.zeros((V_DIM, cw), BF16)
    ones = jnp.ones((2 * SUBLANES, tk), BF16)
    for p in range(npiece):
        piece = qt_ref[p]
        qs_refs[p][...] = jnp.where(row < HEAD_DIM, piece, zero)
        qs_refs[npiece + p][...] = jnp.where(row < HEAD_DIM, zero, piece)
    for c in range(nchain):
        m_refs[c][...] = jnp.full((1, cw), -jnp.inf, F32)
        l_refs[c][...] = jnp.zeros((1, cw), F32)
        acc_refs[c][...] = jnp.zeros((V_DIM, cw), F32)

    def scores(step):
        j, c, _ = step
        kt = k_ref[pl.ds(pl.multiple_of(j * tk, tk), tk), :]
        return _dot(kt, qs_refs[c][...])

    def update(step, s):
        j, c, diag = step
        if diag is not None:
            kpos = diag * tk + lax.broadcasted_iota(jnp.int32, s.shape, 0)
            qpos = (c % npiece) * cw + lax.broadcasted_iota(jnp.int32, s.shape, 1)
            s = jnp.where(kpos <= qpos, s, -jnp.inf)
        m_old = m_refs[c][...]
        m_new = jnp.maximum(m_old, jnp.max(s, axis=0, keepdims=True))
        alpha = jnp.exp2(m_old - m_new)
        p = jnp.exp2(s - m_new).astype(BF16)
        vt = jnp.concatenate([vt_ref[j * kp + t] for t in range(kp)], axis=1)
        pv = _dot(jnp.concatenate([vt, ones], axis=0), p)
        l_refs[c][...] = alpha * l_refs[c][...] + pv[V_DIM:V_DIM + 1]
        acc_refs[c][...] = alpha * acc_refs[c][...] + pv[0:V_DIM]
        m_refs[c][...] = m_new

    def run(steps):
        pending = [scores(st) for st in steps[:LOOKAHEAD]]
        for n, st in enumerate(steps):
            s = pending.pop(0)
            update(st, s)
            if n + LOOKAHEAD < len(steps):
                pending.append(scores(steps[n + LOOKAHEAD]))

    def body(it, carry):
        run([(it * ntile + t, c, None) for t in range(ntile) for c in range(nchain)])
        return carry

    if seq > tq:
        lax.fori_loop(0, i, body, 0)
    todo = {c: [] for c in range(nchain)}
    for t in range(ntile):
        for c in range(nchain):
            q_lo = (c % npiece) * cw
            if t * tk <= q_lo + cw - 1:
                todo[c].append((i * ntile + t, c, t if (t + 1) * tk - 1 > q_lo else None))
    diag_steps, last_at = [], {c: -CHAIN_GAP for c in todo}
    while any(todo.values()):
        live = [c for c in todo if todo[c]]
        rested = [c for c in live if len(diag_steps) - last_at[c] >= CHAIN_GAP]
        c = max(rested or live, key=lambda c: (len(todo[c]), -last_at[c]))
        last_at[c] = len(diag_steps)
        diag_steps.append(todo[c].pop(0))
    run(diag_steps)

    lq = lq_ref[...]
    lk = lk_ref[...]
    lam = (jnp.exp(jnp.sum(lq[0:1] * lk[0:1], axis=-1, keepdims=True))
           - jnp.exp(jnp.sum(lq[1:2] * lk[1:2], axis=-1, keepdims=True)) + lambda_init)
    gain = sub_ref[...] * (1.0 - lambda_init)
    for p in range(npiece):
        w0 = 1.0 / l_refs[p][...]
        w1 = lam / l_refs[npiece + p][...]
        o = acc_refs[p][...] * w0 - acc_refs[npiece + p][...] * w1
        ms = jnp.mean(o * o, axis=0, keepdims=True)
        o = o * lax.rsqrt(ms + EPS) * gain
        o_ref[p * cw:(p + 1) * cw, :] = o.T.astype(BF16)


def _attn(qt, k, vt, lam_q, lam_k, subln, *, layer, batch, seq, tq, tk, lambda_init):
    d = k.shape[-1]
    cw = qt.shape[-1]
    npc = seq // cw
    nchain = 2 * tq // cw
    qt4 = qt.reshape(batch, npc, d, cw)
    vt4 = vt.reshape(batch, npc, d, cw)
    k3 = k.reshape(batch, seq, d)
    qspec = pl.BlockSpec((None, tq // cw, V_DIM, cw), lambda b, h, i: (b, i, h, 0))
    kspec = pl.BlockSpec((None, seq, V_DIM), lambda b, h, i: (b, 0, h))
    vspec = pl.BlockSpec((None, npc, V_DIM, cw), lambda b, h, i: (b, 0, h, 0))
    ospec = pl.BlockSpec((None, tq, V_DIM), lambda b, h, i: (b, i, h))
    small = lambda shape: pl.BlockSpec((None,) + shape, lambda b, h, i: (layer, 0, 0))
    out = pl.pallas_call(
        functools.partial(_attn_kernel, seq=seq, tq=tq, tk=tk, cw=cw, lambda_init=lambda_init),
        grid=(batch, N_HEADS, seq // tq),
        in_specs=[qspec, kspec, vspec, small((2, HEAD_DIM)), small((2, HEAD_DIM)),
                  small((V_DIM, 1))],
        out_specs=ospec,
        out_shape=jax.ShapeDtypeStruct((batch, seq, d), BF16),
        scratch_shapes=([pltpu.VMEM((V_DIM, cw), BF16)] * nchain + [pltpu.VMEM((1, cw), F32)] * nchain
                        + [pltpu.VMEM((1, cw), F32)] * nchain
                        + [pltpu.VMEM((V_DIM, cw), F32)] * nchain),
        compiler_params=_params("parallel", "parallel", "arbitrary"),
        name="attn",
    )(qt4, k3, vt4, lam_q, lam_k, subln)
    return out.reshape(batch * seq, d)


def kernel(x, ffn1_norm, ffn1_w_in, ffn1_w_out, mix_norm, w_in, conv_dw, conv_dw_b, conv_ln_g,
           conv_ln_b, conv_w_out, q_norm, k_norm, lam_q, lam_k, attn_subln, attn_w_out, w_out,
           ffn2_norm, ffn2_w_in, ffn2_w_out):
    batch, seq, d = x.shape
    depth = w_in.shape[0]
    assert d == D_MODEL and seq % 512 == 0
    xs = x.reshape(batch * seq, d)
    rows = lambda p: p.reshape(depth, 1, -1)
    b16 = lambda p: p.astype(BF16)
    ffn1_w_in, ffn1_w_out, w_in = b16(ffn1_w_in), b16(ffn1_w_out), b16(w_in)
    conv_w_out, attn_w_out, w_out = b16(conv_w_out), b16(attn_w_out), b16(w_out)
    ffn2_w_in, ffn2_w_out = b16(ffn2_w_in), b16(ffn2_w_out)
    qg = rows(jnp.tile(q_norm, (1, 2 * N_HEADS)))
    kg = rows(jnp.tile(k_norm, (1, 2 * N_HEADS)))
    subln = attn_subln.reshape(depth, V_DIM, 1)
    for l in range(depth):
        lambda_init = 0.8 - 0.6 * math.exp(-0.3 * l)
        xs = _ffn(xs, rows(ffn1_norm), ffn1_w_in, ffn1_w_out, layer=l, tm=FFN_TM)
        u, qt, k, vt, gc, ga = _proj(xs, rows(mix_norm), w_in, qg, kg, layer=l, tm=ATTN_CW)
        attn_o = _attn(qt, k, vt, lam_q, lam_k, subln, layer=l, batch=batch, seq=seq,
                       tq=ATTN_TQ, tk=ATTN_TK, lambda_init=lambda_init)
        xs = _tail(u, xs, attn_o, gc, ga, conv_dw, rows(conv_dw_b), rows(conv_ln_g),
                   rows(conv_ln_b), conv_w_out, attn_w_out, w_out, rows(ffn2_norm), ffn2_w_in,
                   ffn2_w_out, layer=l, seq=seq, tt=TAIL_TT, rc=CONV_RC)
    return xs.reshape(batch, seq, d)
```

```python
import functools
import math

import jax
import jax.numpy as jnp
from jax import lax
from jax.experimental import pallas as pl
from jax.experimental.pallas import tpu as pltpu

D_MODEL = 1024
N_HEADS = 8
HEAD_DIM = 64
V_DIM = 2 * HEAD_DIM
CONV_K = 31
EPS = 1e-6

VMEM_LIMIT_BYTES = 56 * 1024 * 1024
SUBLANES, LANES = 8, 128
FFN_CHUNK = 256
HALO = 32
FFN_TM = 512
TAIL_TT = 256
CONV_RC = 64
ATTN_TQ = 2048
ATTN_TK = 256
ATTN_CW = 256
LOOKAHEAD = 6
CHAIN_GAP = 8
Q_SCALE = HEAD_DIM ** -0.5 * math.log2(math.e)

F32 = jnp.float32
BF16 = jnp.bfloat16


def _params(*sem):
    return pltpu.CompilerParams(dimension_semantics=sem, vmem_limit_bytes=VMEM_LIMIT_BYTES)


def _layer_spec(shape, layer, col=0):
    idx = (layer,) + (0,) * (len(shape) - 1) + (col,)
    return pl.BlockSpec((None,) + tuple(shape), lambda *_: idx, pipeline_mode=pl.Buffered(1))


def _rms_rows(x, g):
    ms = jnp.mean(x * x, axis=-1, keepdims=True)
    return x * lax.rsqrt(ms + EPS) * g


def _dot(a, b):
    return jnp.dot(a, b, preferred_element_type=F32)


def _ffn_chunk(h, c, wa_ref, wb_ref, wo_ref, acc_ref):
    sl = slice(c * FFN_CHUNK, (c + 1) * FFN_CHUNK)
    a = _dot(h, wa_ref[:, sl])
    b = _dot(h, wb_ref[:, sl])
    gated = (a * jax.nn.sigmoid(a) * b).astype(BF16)
    contrib = _dot(gated, wo_ref[sl, :])
    if c == 0:
        acc_ref[...] = contrib
    else:
        acc_ref[...] += contrib


def _ffn_kernel(x_ref, g_ref, wa_ref, wb_ref, wo_ref, o_ref, acc_ref, *, d_ff):
    x = x_ref[...]
    h = _rms_rows(x, g_ref[...]).astype(BF16)
    for c in range(d_ff // FFN_CHUNK):
        _ffn_chunk(h, c, wa_ref, wb_ref, wo_ref, acc_ref)
    o_ref[...] = x + 0.5 * acc_ref[...]


def _ffn(x, g, w_in, w_out, *, layer, tm):
    n, d = x.shape
    d_ff = w_out.shape[1]
    row = pl.BlockSpec((tm, d), lambda i: (i, 0))
    return pl.pallas_call(
        functools.partial(_ffn_kernel, d_ff=d_ff),
        grid=(n // tm,),
        in_specs=[
            row,
            _layer_spec((1, d), layer),
            _layer_spec((d, d_ff), layer, 0),
            _layer_spec((d, d_ff), layer, 1),
            _layer_spec((d_ff, d), layer),
        ],
        out_specs=row,
        out_shape=jax.ShapeDtypeStruct((n, d), F32),
        scratch_shapes=[pltpu.VMEM((tm, d), F32)],
        compiler_params=_params("parallel"),
        name="ffn",
    )(x, g, w_in, w_in, w_out)


def _qk_norm(p, gain):
    outs = []
    for hh in range(N_HEADS):
        slab = p[:, hh * V_DIM:(hh + 1) * V_DIM]
        sq = slab * slab
        lane = lax.broadcasted_iota(jnp.int32, slab.shape, 1)
        first = lane < HEAD_DIM
        s0 = jnp.sum(jnp.where(first, sq, 0.0), axis=-1, keepdims=True)
        s1 = jnp.sum(jnp.where(first, 0.0, sq), axis=-1, keepdims=True)
        inv = jnp.where(first, lax.rsqrt(s0 / HEAD_DIM + EPS), lax.rsqrt(s1 / HEAD_DIM + EPS))
        outs.append(slab * inv * gain[:, hh * V_DIM:(hh + 1) * V_DIM])
    return outs


def _proj_kernel(x_ref, g_ref, w_ref, qg_ref, kg_ref,
                 u_ref, qt_ref, k_ref, vt_ref, gc_ref, ga_ref):
    d = D_MODEL
    h = _rms_rows(x_ref[...], g_ref[...]).astype(BF16)
    a = _dot(h, w_ref[:, 0:d])
    gte = _dot(h, w_ref[:, d:2 * d])
    u_ref[...] = a * jax.nn.sigmoid(gte)
    q = _dot(h, w_ref[:, 2 * d:3 * d])
    for hh, qh in enumerate(_qk_norm(q, qg_ref[...])):
        qt_ref[hh * V_DIM:(hh + 1) * V_DIM, :] = (qh * Q_SCALE).T.astype(BF16)
    k = _dot(h, w_ref[:, 3 * d:4 * d])
    for hh, kh in enumerate(_qk_norm(k, kg_ref[...])):
        k_ref[:, hh * V_DIM:(hh + 1) * V_DIM] = kh.astype(BF16)
    vt_ref[...] = _dot(h, w_ref[:, 4 * d:5 * d]).T.astype(BF16)
    gc_ref[...] = jax.nn.sigmoid(_dot(h, w_ref[:, 5 * d:6 * d]))
    ga_ref[...] = jax.nn.sigmoid(_dot(h, w_ref[:, 6 * d:7 * d]))


def _proj(x, g, w, qg, kg, *, layer, tm):
    n, d = x.shape
    row = pl.BlockSpec((tm, d), lambda i: (i, 0))
    tr = pl.BlockSpec((None, d, tm), lambda i: (i, 0, 0))
    f32o = jax.ShapeDtypeStruct((n, d), F32)
    b16o = jax.ShapeDtypeStruct((n, d), BF16)
    tro = jax.ShapeDtypeStruct((n // tm, d, tm), BF16)
    return pl.pallas_call(
        _proj_kernel,
        grid=(n // tm,),
        in_specs=[row, _layer_spec((1, d), layer), _layer_spec(w.shape[1:], layer),
                  _layer_spec((1, d), layer), _layer_spec((1, d), layer)],
        out_specs=[row, tr, row, tr, row, row],
        out_shape=[f32o, tro, b16o, tro, f32o, f32o],
        compiler_params=_params("parallel"),
        name="proj",
    )(x, g, w, qg, kg)


def _conv_units(win_ref, wsh_ref, dw_ref, b_ref, lg_ref, lb_ref, y_ref, act_ref, *, tt, rc):
    first = HALO - (CONV_K - 1)
    rows = HALO + tt

    def shift(lanes, phase, after=None):
        del after
        moved = pltpu.roll(win_ref[:, lanes], rows - phase, axis=0)
        wsh_ref[phase - 1, :, lanes] = moved
        return moved[0:SUBLANES]

    def depthwise(r0, lanes, after=None):
        bias = jnp.broadcast_to(b_ref[:, lanes], (SUBLANES, LANES))
        if False and after is not None:
            bias = _after(bias, [after])
        acc = jnp.concatenate([bias] * (rc // SUBLANES), axis=0)
        for tap in range(CONV_K):
            a, phase = divmod(first + tap, SUBLANES)
            lo = r0 + SUBLANES * a
            src = win_ref[lo:lo + rc, lanes] if phase == 0 else wsh_ref[phase - 1, lo:lo + rc, lanes]
            acc = acc + dw_ref[tap:tap + 1, lanes] * src
        y_ref[r0:r0 + rc, lanes] = acc
        return acc[0:SUBLANES]

    def norm_act(rows, after=None):
        del after
        y = y_ref[rows, :]
        mu = jnp.mean(y, axis=-1, keepdims=True)
        yc = y - mu
        var = jnp.mean(yc * yc, axis=-1, keepdims=True)
        y = yc * lax.rsqrt(var + EPS) * lg_ref[...] + lb_ref[...]
        y = y * jax.nn.sigmoid(y)
        act_ref[rows, :] = y.astype(BF16)
        return y[0:SUBLANES, 0:LANES]

    units = []
    for lt in range(D_MODEL // LANES):
        lanes = slice(lt * LANES, (lt + 1) * LANES)
        units += [functools.partial(shift, lanes, p) for p in range(1, SUBLANES)]
        units += [functools.partial(depthwise, c * rc, lanes) for c in range(tt // rc)]
    units += [functools.partial(norm_act, slice(c * rc, (c + 1) * rc)) for c in range(tt // rc)]
    return units


def _zero_bits(values):
    zero = None
    for v in values:
        bits = pltpu.bitcast(v, jnp.uint32)
        z = lax.shift_right_logical(lax.shift_right_logical(bits, jnp.uint32(16)), jnp.uint32(16))
        zero = z if zero is None else zero | z
    return zero


def _after(x, values):
    return pltpu.bitcast(pltpu.bitcast(x, jnp.uint32) + _zero_bits(values), x.dtype)


def _order_after(ref, values):
    if False and values:
        rows = SUBLANES * 4 // ref.dtype.itemsize
        ref[0:rows, 0:LANES] = _after(ref[0:rows, 0:LANES], values)


def _tail_kernel(u_ref, halo_ref, dw_ref, b_ref, lg_ref, lb_ref,
                 x_ref, a_ref, gc_ref, ga_ref, wc_ref, wa_ref, wo_ref,
                 fg_ref, fwa_ref, fwb_ref, fwo_ref,
                 o_ref, win_ref, wsh_ref, y_ref, act_ref, acc_ref, mg_ref, h_ref, gt_ref,
                 *, tt, rc, d_ff, tiles_per_seq):
    s = pl.program_id(0)
    d = D_MODEL

    @pl.when(s == 0)
    def _():
        act_ref[...] = jnp.zeros(act_ref.shape, BF16)

    prev_act = act_ref[...]
    win_ref[0:HALO, :] = jnp.where(s % tiles_per_seq != 0, halo_ref[...], 0.0)
    win_ref[HALO:HALO + tt, :] = u_ref[...]
    conv = _conv_units(win_ref, wsh_ref, dw_ref, b_ref, lg_ref, lb_ref, y_ref, act_ref, tt=tt, rc=rc)
    nchunk = d_ff // FFN_CHUNK
    cover = [2 * d * d, d * d] + [2 * d * FFN_CHUNK, FFN_CHUNK * d] * nchunk
    edges = [round(len(conv) * sum(cover[:g]) / sum(cover)) for g in range(len(cover) + 1)]
    groups = iter([conv[lo:hi] for lo, hi in zip(edges[:-1], edges[1:])])
    tile = lambda v: v[0:SUBLANES, 0:LANES]
    run_group = lambda after: [unit(after=after) for unit in next(groups)]

    yc = _dot(prev_act, wc_ref[...])
    ya = _dot(a_ref[...], wa_ref[...])
    mg_ref[...] = (gc_ref[...] * yc + ga_ref[...] * ya).astype(BF16)
    _order_after(mg_ref, run_group(None))
    proj = _dot(mg_ref[...], wo_ref[...])
    x = x_ref[...] + proj
    h_ref[...] = _rms_rows(x, fg_ref[...]).astype(BF16)
    _order_after(h_ref, run_group(tile(ya)))
    last = tile(proj)
    for c in range(nchunk):
        sl = slice(c * FFN_CHUNK, (c + 1) * FFN_CHUNK)
        h = h_ref[...]
        a = _dot(h, fwa_ref[:, sl])
        b = _dot(h, fwb_ref[:, sl])
        gt_ref[...] = (a * jax.nn.sigmoid(a) * b).astype(BF16)
        _order_after(gt_ref, run_group(last))
        contrib = _dot(gt_ref[...], fwo_ref[sl, :])
        if c == 0:
            acc_ref[...] = contrib
        else:
            acc_ref[...] += contrib
        _order_after(acc_ref, run_group(tile(b)))
        last = tile(contrib)
    o_ref[...] = x + 0.5 * acc_ref[...]


def _tail(u, x, attn_o, gc, ga, dw, dw_b, ln_g, ln_b, wc, wa, wo, fg, fw_in, fw_out,
          *, layer, seq, tt, rc):
    n, d = x.shape
    d_ff = fw_out.shape[1]
    vec = _layer_spec((1, d), layer)
    mat = _layer_spec((d, d), layer)
    nt = n // tt
    per = tt // HALO
    conv_tile = lambda s: jnp.minimum(s, nt - 1)
    cur = pl.BlockSpec((tt, d), lambda s: (conv_tile(s), 0))
    halo = pl.BlockSpec((HALO, d), lambda s: (jnp.maximum(conv_tile(s) * per - 1, 0), 0))
    prev = pl.BlockSpec((tt, d), lambda s: (jnp.maximum(s - 1, 0), 0))
    return pl.pallas_call(
        functools.partial(_tail_kernel, tt=tt, rc=rc, d_ff=d_ff, tiles_per_seq=seq // tt),
        grid=(nt + 1,),
        in_specs=[cur, halo, _layer_spec((CONV_K, d), layer), vec, vec, vec,
                  prev, prev, prev, prev, mat, mat, mat,
                  vec, _layer_spec((d, d_ff), layer, 0), _layer_spec((d, d_ff), layer, 1),
                  _layer_spec((d_ff, d), layer)],
        out_specs=prev,
        out_shape=jax.ShapeDtypeStruct((n, d), F32),
        scratch_shapes=[pltpu.VMEM((HALO + tt, d), F32),
                        pltpu.VMEM((SUBLANES - 1, HALO + tt, d), F32), pltpu.VMEM((tt, d), F32),
                        pltpu.VMEM((tt, d), BF16), pltpu.VMEM((tt, d), F32),
                        pltpu.VMEM((tt, d), BF16), pltpu.VMEM((tt, d), BF16),
                        pltpu.VMEM((tt, FFN_CHUNK), BF16)],
        compiler_params=_params("arbitrary"),
        name="tail",
    )(u, u, dw, dw_b, ln_g, ln_b, x, attn_o, gc, ga, wc, wa, wo, fg, fw_in, fw_in, fw_out)


def _attn_kernel(qt_ref, k_ref, vt_ref, lq_ref, lk_ref, sub_ref, o_ref, *scratch,
                 seq, tq, tk, cw, lambda_init):
    i = pl.program_id(2)
    npiece = tq // cw
    nchain = 2 * npiece
    kp = tk // cw
    ntile = tq // tk
    qs_refs, m_refs, l_refs, acc_refs = (scratch[n * nchain:(n + 1) * nchain] for n in range(4))
    row = lax.broadcasted_iota(jnp.int32, (V_DIM, cw), 0)
    zero = jnp.zeros((V_DIM, cw), BF16)
    ones = jnp.ones((2 * SUBLANES, tk), BF16)
    for p in range(npiece):
        piece = qt_ref[p]
        qs_refs[p][...] = jnp.where(row < HEAD_DIM, piece, zero)
        qs_refs[npiece + p][...] = jnp.where(row < HEAD_DIM, zero, piece)
    for c in range(nchain):
        m_refs[c][...] = jnp.full((1, cw), -jnp.inf, F32)
        l_refs[c][...] = jnp.zeros((1, cw), F32)
        acc_refs[c][...] = jnp.zeros((V_DIM, cw), F32)

    def scores(step):
        j, c, _ = step
        kt = k_ref[pl.ds(pl.multiple_of(j * tk, tk), tk), :]
        return _dot(kt, qs_refs[c][...])

    def update(step, s):
        j, c, diag = step
        if diag is not None:
            kpos = diag * tk + lax.broadcasted_iota(jnp.int32, s.shape, 0)
            qpos = (c % npiece) * cw + lax.broadcasted_iota(jnp.int32, s.shape, 1)
            s = jnp.where(kpos <= qpos, s, -jnp.inf)
        m_old = m_refs[c][...]
        m_new = jnp.maximum(m_old, jnp.max(s, axis=0, keepdims=True))
        alpha = jnp.exp2(m_old - m_new)
        p = jnp.exp2(s - m_new).astype(BF16)
        vt = jnp.concatenate([vt_ref[j * kp + t] for t in range(kp)], axis=1)
        pv = _dot(jnp.concatenate([vt, ones], axis=0), p)
        l_refs[c][...] = alpha * l_refs[c][...] + pv[V_DIM:V_DIM + 1]
        acc_refs[c][...] = alpha * acc_refs[c][...] + pv[0:V_DIM]
        m_refs[c][...] = m_new

    def run(steps):
        pending = [scores(st) for st in steps[:LOOKAHEAD]]
        for n, st in enumerate(steps):
            s = pending.pop(0)
            update(st, s)
            if n + LOOKAHEAD < len(steps):
                pending.append(scores(steps[n + LOOKAHEAD]))

    def body(it, carry):
        run([(it * ntile + t, c, None) for t in range(ntile) for c in range(nchain)])
        return carry

    if seq > tq:
        lax.fori_loop(0, i, body, 0)
    todo = {c: [] for c in range(nchain)}
    for t in range(ntile):
        for c in range(nchain):
            q_lo = (c % npiece) * cw
            if t * tk <= q_lo + cw - 1:
                todo[c].append((i * ntile + t, c, t if (t + 1) * tk - 1 > q_lo else None))
    diag_steps, last_at = [], {c: -CHAIN_GAP for c in todo}
    while any(todo.values()):
        live = [c for c in todo if todo[c]]
        rested = [c for c in live if len(diag_steps) - last_at[c] >= CHAIN_GAP]
        c = max(rested or live, key=lambda c: (len(todo[c]), -last_at[c]))
        last_at[c] = len(diag_steps)
        diag_steps.append(todo[c].pop(0))
    run(diag_steps)

    lq = lq_ref[...]
    lk = lk_ref[...]
    lam = (jnp.exp(jnp.sum(lq[0:1] * lk[0:1], axis=-1, keepdims=True))
           - jnp.exp(jnp.sum(lq[1:2] * lk[1:2], axis=-1, keepdims=True)) + lambda_init)
    gain = sub_ref[...] * (1.0 - lambda_init)
    for p in range(npiece):
        w0 = 1.0 / l_refs[p][...]
        w1 = lam / l_refs[npiece + p][...]
        o = acc_refs[p][...] * w0 - acc_refs[npiece + p][...] * w1
        ms = jnp.mean(o * o, axis=0, keepdims=True)
        o = o * lax.rsqrt(ms + EPS) * gain
        o_ref[p * cw:(p + 1) * cw, :] = o.T.astype(BF16)


def _attn(qt, k, vt, lam_q, lam_k, subln, *, layer, batch, seq, tq, tk, lambda_init):
    d = k.shape[-1]
    cw = qt.shape[-1]
    npc = seq // cw
    nchain = 2 * tq // cw
    qt4 = qt.reshape(batch, npc, d, cw)
    vt4 = vt.reshape(batch, npc, d, cw)
    k3 = k.reshape(batch, seq, d)
    qspec = pl.BlockSpec((None, tq // cw, V_DIM, cw), lambda b, h, i: (b, i, h, 0))
    kspec = pl.BlockSpec((None, seq, V_DIM), lambda b, h, i: (b, 0, h))
    vspec = pl.BlockSpec((None, npc, V_DIM, cw), lambda b, h, i: (b, 0, h, 0))
    ospec = pl.BlockSpec((None, tq, V_DIM), lambda b, h, i: (b, i, h))
    small = lambda shape: pl.BlockSpec((None,) + shape, lambda b, h, i: (layer, 0, 0))
    out = pl.pallas_call(
        functools.partial(_attn_kernel, seq=seq, tq=tq, tk=tk, cw=cw, lambda_init=lambda_init),
        grid=(batch, N_HEADS, seq // tq),
        in_specs=[qspec, kspec, vspec, small((2, HEAD_DIM)), small((2, HEAD_DIM)),
                  small((V_DIM, 1))],
        out_specs=ospec,
        out_shape=jax.ShapeDtypeStruct((batch, seq, d), BF16),
        scratch_shapes=([pltpu.VMEM((V_DIM, cw), BF16)] * nchain + [pltpu.VMEM((1, cw), F32)] * nchain
                        + [pltpu.VMEM((1, cw), F32)] * nchain
                        + [pltpu.VMEM((V_DIM, cw), F32)] * nchain),
        compiler_params=_params("parallel", "parallel", "arbitrary"),
        name="attn",
    )(qt4, k3, vt4, lam_q, lam_k, subln)
    return out.reshape(batch * seq, d)


def kernel(x, ffn1_norm, ffn1_w_in, ffn1_w_out, mix_norm, w_in, conv_dw, conv_dw_b, conv_ln_g,
           conv_ln_b, conv_w_out, q_norm, k_norm, lam_q, lam_k, attn_subln, attn_w_out, w_out,
           ffn2_norm, ffn2_w_in, ffn2_w_out):
    batch, seq, d = x.shape
    depth = w_in.shape[0]
    assert d == D_MODEL and seq % 512 == 0
    xs = x.reshape(batch * seq, d)
    rows = lambda p: p.reshape(depth, 1, -1)
    b16 = lambda p: p.astype(BF16)
    ffn1_w_in, ffn1_w_out, w_in = b16(ffn1_w_in), b16(ffn1_w_out), b16(w_in)
    conv_w_out, attn_w_out, w_out = b16(conv_w_out), b16(attn_w_out), b16(w_out)
    ffn2_w_in, ffn2_w_out = b16(ffn2_w_in), b16(ffn2_w_out)
    qg = rows(jnp.tile(q_norm, (1, 2 * N_HEADS)))
    kg = rows(jnp.tile(k_norm, (1, 2 * N_HEADS)))
    subln = attn_subln.reshape(depth, V_DIM, 1)
    for l in range(depth):
        lambda_init = 0.8 - 0.6 * math.exp(-0.3 * l)
        xs = _ffn(xs, rows(ffn1_norm), ffn1_w_in, ffn1_w_out, layer=l, tm=FFN_TM)
        u, qt, k, vt, gc, ga = _proj(xs, rows(mix_norm), w_in, qg, kg, layer=l, tm=ATTN_CW)
        attn_o = _attn(qt, k, vt, lam_q, lam_k, subln, layer=l, batch=batch, seq=seq,
                       tq=ATTN_TQ, tk=ATTN_TK, lambda_init=lambda_init)
        xs = _tail(u, xs, attn_o, gc, ga, conv_dw, rows(conv_dw_b), rows(conv_ln_g),
                   rows(conv_ln_b), conv_w_out, attn_w_out, w_out, rows(ffn2_norm), ffn2_w_in,
                   ffn2_w_out, layer=l, seq=seq, tt=TAIL_TT, rc=CONV_RC)
    return xs.reshape(batch, seq, d)
```
